```python
import math
import jax
import jax.numpy as jnp
from jax import lax
import numpy as np

D_MODEL = 1024
BATCH = 4
SEQ = 4096
DEPTH = 2
DEC_BATCH = 128
DEC_SEQ = 1
PAST_LEN = 8192
PAGE_SIZE = 128

EPS = 1e-6
MLA_HEADS = 8
MLA_NOPE = 64
MLA_ROPE = 32
MLA_QK = MLA_NOPE + MLA_ROPE
MLA_V = 64
MLA_Q_LORA = 384
MLA_KV_LORA = 256
MLA_THETA = 10000.0
MOBA_HEADS = 8
MOBA_HD = 64
MOBA_ROT = MOBA_HD // 4
MOBA_BLOCK = 256
MOBA_TOPK = 3
MOBA_Q_CHUNK = 32
ROPE_THETA = 500000.0
ATTN_Q_BLOCK = 128
GLA_HEADS = 4
GLA_DK = 128
GLA_DV = 192
GLA_GATE_RANK = 16
GLA_TAU = 16.0
GLA_CHUNK = 64
POOL_WINDOWS = (2, 4, 8, 16)
POOL_GDIM = 64
POOL_DIM = POOL_GDIM * len(POOL_WINDOWS)
POOL_HIST = max(POOL_WINDOWS) - 1
MEM_LEN = 256
MEM_HEADS = 4
MEM_HD = 128
D_FF = 4 * D_MODEL

N_EVEN = (DEPTH + 1) // 2
N_ODD = DEPTH // 2
EVEN_IN = MLA_Q_LORA + MLA_KV_LORA + MLA_ROPE + 3 * MOBA_HEADS * MOBA_HD
EVEN_MIX = MLA_HEADS * MLA_V + MOBA_HEADS * MOBA_HD
ODD_IN = 2 * GLA_HEADS * GLA_DK + 2 * GLA_HEADS * GLA_DV + GLA_GATE_RANK + POOL_DIM
ODD_MIX = GLA_HEADS * GLA_DV + POOL_DIM

kernel_name = 'hybrid_mla_moba_gla_pool_step'


def rms_norm(x, g):
    xf = x.astype(jnp.float32)
    y = xf * lax.rsqrt(jnp.mean(xf * xf, axis=-1, keepdims=True) + EPS)
    return (y * g.astype(jnp.float32)).astype(x.dtype)


def rope(x, pos, theta):
    half = x.shape[-1] // 2
    inv = jnp.exp(jnp.arange(half, dtype=jnp.float32) * (-math.log(theta) / half))
    ang = pos.astype(jnp.float32)[:, None] * inv[None, :]
    cos = jnp.cos(ang)[:, None, :]
    sin = jnp.sin(ang)[:, None, :]
    xf = x.astype(jnp.float32)
    x1, x2 = xf[..., :half], xf[..., half:]
    return jnp.concatenate([x1 * cos - x2 * sin, x2 * cos + x1 * sin], -1).astype(x.dtype)


def partial_rope(x, pos):
    return jnp.concatenate([rope(x[..., :MOBA_ROT], pos, ROPE_THETA), x[..., MOBA_ROT:]], -1)


def mla_queries(cq, w_uq, g_q, pos):
    q = rms_norm(jnp.einsum('...lc,chd->...lhd', cq, w_uq), g_q)
    return jnp.concatenate([q[..., :MLA_NOPE], rope(q[..., MLA_NOPE:], pos, MLA_THETA)], -1)


def mla_keys(ckv, kr, pos, w_uk, g_k):
    k_nope = jnp.einsum('...lc,chd->...lhd', ckv, w_uk)
    k_rope = jnp.broadcast_to(kr[..., None, :], k_nope.shape[:-1] + (MLA_ROPE,))
    k = rms_norm(jnp.concatenate([k_nope, k_rope], -1), g_k)
    return jnp.concatenate([k[..., :MLA_NOPE], rope(k[..., MLA_NOPE:], pos, MLA_THETA)], -1)


def dense_causal_prompt(q, k, v):
    n, s, h, d = q.shape
    nb = s // ATTN_Q_BLOCK
    kpos = jnp.arange(s)
    qb = q.reshape(n, nb, ATTN_Q_BLOCK, h, d).swapaxes(0, 1)

    def one(args):
        qi, bi = args
        qpos = bi * ATTN_Q_BLOCK + jnp.arange(ATTN_Q_BLOCK)
        sc = jnp.einsum('nqhd,nkhd->nhqk', qi, k).astype(jnp.float32) * d ** -0.5
        sc = jnp.where(kpos[None, None, None, :] <= qpos[None, None, :, None], sc, -jnp.inf)
        p = jax.nn.softmax(sc, axis=-1).astype(v.dtype)
        return jnp.einsum('nhqk,nkhd->nqhd', p, v)

    o = lax.map(one, (qb, jnp.arange(nb)))
    return o.swapaxes(0, 1).reshape(n, s, h, v.shape[-1])


def moba_blocks(k, v):
    n, l, h, d = k.shape
    nbt = -(-l // MOBA_BLOCK)
    pad = nbt * MOBA_BLOCK - l

    def blk(t):
        t = jnp.pad(t, ((0, 0), (0, pad), (0, 0), (0, 0)))
        return t.reshape(n, nbt, MOBA_BLOCK, h, d).transpose(0, 3, 1, 2, 4)

    kb, vb = blk(k), blk(v)
    kmean = jnp.mean(kb.astype(jnp.float32), axis=3)
    return kb, vb, kmean


def moba_attend(q, qpos, kb, vb, kmean):
    n, h, nbt, blk, d = kb.shape
    nq = q.shape[1]
    kk = min(MOBA_TOPK, nbt)
    qblk = qpos // MOBA_BLOCK
    gate = jnp.einsum('nqhd,nhjd->nhqj', q.astype(jnp.float32), kmean)
    gate = jnp.where(jnp.arange(nbt)[None, :] < qblk[:, None], gate, -jnp.inf)
    _, idx = lax.top_k(gate, kk)
    n_ix = jnp.arange(n)[:, None, None, None]
    h_ix = jnp.arange(h)[None, :, None, None]
    k_sel = kb[n_ix, h_ix, idx]
    v_sel = vb[n_ix, h_ix, idx]
    k_own = kb[:, :, qblk]
    v_own = vb[:, :, qblk]
    s_sel = jnp.einsum('nqhd,nhqibd->nhqib', q, k_sel).reshape(n, h, nq, kk * blk)
    s_own = jnp.einsum('nqhd,nhqbd->nhqb', q, k_own)
    sel_ok = jnp.repeat(jnp.arange(kk)[None, :] < qblk[:, None], blk, axis=1)
    own_ok = qblk[:, None] * blk + jnp.arange(blk)[None, :] <= qpos[:, None]
    ok = jnp.concatenate([sel_ok, own_ok], -1)
    logits = jnp.concatenate([s_sel, s_own], -1).astype(jnp.float32) * d ** -0.5
    p = jax.nn.softmax(jnp.where(ok[None, None], logits, -jnp.inf), axis=-1).astype(vb.dtype)
    p_sel = p[..., :kk * blk].reshape(n, h, nq, kk, blk)
    p_own = p[..., kk * blk:]
    return (jnp.einsum('nhqib,nhqibd->nqhd', p_sel, v_sel)
            + jnp.einsum('nhqb,nhqbd->nqhd', p_own, v_own))


def moba_prompt(q, k, v):
    n, s, h, d = q.shape
    kb, vb, kmean = moba_blocks(k, v)
    nc = s // MOBA_Q_CHUNK
    qc = q.reshape(n, nc, MOBA_Q_CHUNK, h, d).swapaxes(0, 1)

    def one(args):
        qi, ci = args
        return moba_attend(qi, ci * MOBA_Q_CHUNK + jnp.arange(MOBA_Q_CHUNK), kb, vb, kmean)

    o = lax.map(one, (qc, jnp.arange(nc)))
    return o.swapaxes(0, 1).reshape(n, s, h, d)


def gla_chunk(st, q, k, v, g):
    qf, kf, vf = q.astype(jnp.float32), k.astype(jnp.float32), v.astype(jnp.float32)
    gc = jnp.cumsum(g, axis=1)
    c = q.shape[1]
    o_inter = jnp.einsum('nchk,nhkv->nchv', qf * jnp.exp(gc), st)
    causal = jnp.tril(jnp.ones((c, c), bool))[None, :, :, None, None]
    decay = jnp.exp(jnp.where(causal, gc[:, :, None] - gc[:, None, :], -jnp.inf))
    a = jnp.einsum('nthk,nshk,ntshk->nhts', qf, kf, decay)
    o_intra = jnp.einsum('nhts,nshv->nthv', a, vf)
    g_last = gc[:, -1]
    st_new = st * jnp.exp(g_last)[..., None] + jnp.einsum(
        'nshk,nshv->nhkv', kf * jnp.exp(g_last[:, None] - gc), vf)
    return st_new, o_inter + o_intra


def gla_prompt(q, k, v, g):
    n, s, h, dk = q.shape
    nc = s // GLA_CHUNK

    def chunks(t):
        return t.reshape((n, nc, GLA_CHUNK) + t.shape[2:]).swapaxes(0, 1)

    s0 = jnp.zeros((n, h, dk, GLA_DV), jnp.float32)
    s_fin, o = lax.scan(lambda st, xs: gla_chunk(st, *xs), s0,
                        (chunks(q), chunks(k), chunks(v), chunks(g)))
    return s_fin, o.swapaxes(0, 1).reshape(n, s, h, GLA_DV)


def pool_mix(u_ext, first_pos, w_pool, scale):
    t = u_ext.shape[1] - POOL_HIST
    uf = u_ext.astype(jnp.float32)
    c = jnp.pad(jnp.cumsum(uf, axis=1), ((0, 0), (1, 0), (0, 0)))
    pos = first_pos + jnp.arange(t)
    outs = []
    for gi, w in enumerate(POOL_WINDOWS):
        lo, hi = gi * POOL_GDIM, (gi + 1) * POOL_GDIM
        wsum = (c[:, POOL_HIST + 1:POOL_HIST + 1 + t, lo:hi]
                - c[:, POOL_HIST + 1 - w:POOL_HIST + 1 - w + t, lo:hi])
        cnt = jnp.minimum(pos + 1, w).astype(jnp.float32)[None, :, None]
        diff = wsum / cnt - uf[:, POOL_HIST:, lo:hi]
        outs.append(jnp.einsum('ntc,cd->ntd', diff, w_pool[gi].astype(jnp.float32)))
    return (jnp.concatenate(outs, -1) * scale.astype(jnp.float32)).astype(u_ext.dtype)


def even_split(z):
    n, l = z.shape[:2]
    i0 = MLA_Q_LORA
    i1 = i0 + MLA_KV_LORA
    i2 = i1 + MLA_ROPE
    qkv = z[..., i2:].reshape(n, l, 3, MOBA_HEADS, MOBA_HD)
    return z[..., :i0], z[..., i0:i1], z[..., i1:i2], qkv[:, :, 0], qkv[:, :, 1], qkv[:, :, 2]


def even_prompt(h, w_in, g_cq, g_ckv, w_uq, w_uk, w_uv, g_aq, g_ak, g_bq, g_bk, w_out):
    n, s, _ = h.shape
    pos = jnp.arange(s)
    cq, ckv, kr, qb, kb, vb = even_split(h @ w_in)
    ckv = rms_norm(ckv, g_ckv)
    qa = mla_queries(rms_norm(cq, g_cq), w_uq, g_aq, pos)
    ka = mla_keys(ckv, kr, pos, w_uk, g_ak)
    va = jnp.einsum('nlc,chv->nlhv', ckv, w_uv)
    oa = dense_causal_prompt(qa, ka, va)
    qb = partial_rope(rms_norm(qb, g_bq), pos)
    kb = partial_rope(rms_norm(kb, g_bk), pos)
    ob = moba_prompt(qb, kb, vb)
    mix = jnp.concatenate([oa.reshape(n, s, -1), ob.reshape(n, s, -1)], -1)
    return mix @ w_out, (ckv, kr, kb, vb)


def even_sample(h, li, page_table, cache_ckv, cache_kr, cache_k, cache_v,
                w_in, g_cq, g_ckv, w_uq, w_uk, w_uv, g_aq, g_ak, g_bq, g_bk, w_out):
    n, t, _ = h.shape
    past = page_table.shape[1] * PAGE_SIZE
    qpos = past + jnp.arange(t)
    kpos = jnp.arange(past + t)
    cq, ckv, kr, qb, kb, vb = even_split(h @ w_in)
    ckv = rms_norm(ckv, g_ckv)
    qa = mla_queries(rms_norm(cq, g_cq), w_uq, g_aq, qpos)
    qb = partial_rope(rms_norm(qb, g_bq), qpos)
    kb = partial_rope(rms_norm(kb, g_bk), qpos)

    def one(args):
        qa_i, ckv_i, kr_i, qb_i, kb_i, vb_i, pages = args
        ckv_all = jnp.concatenate([cache_ckv[li, pages].reshape(past, MLA_KV_LORA), ckv_i], 0)
        kr_all = jnp.concatenate([cache_kr[li, pages].reshape(past, MLA_ROPE), kr_i], 0)
        ka = mla_keys(ckv_all, kr_all, kpos, w_uk, g_ak)
        sc = jnp.einsum('qhd,khd->hqk', qa_i, ka).astype(jnp.float32) * MLA_QK ** -0.5
        sc = jnp.where(kpos[None, None, :] <= qpos[None, :, None], sc, -jnp.inf)
        p = jax.nn.softmax(sc, axis=-1).astype(ckv_all.dtype)
        oa = jnp.einsum('qhc,chv->qhv', jnp.einsum('hqk,kc->qhc', p, ckv_all), w_uv)
        k_all = jnp.concatenate([cache_k[li, pages].reshape(past, MOBA_HEADS, MOBA_HD), kb_i], 0)
        v_all = jnp.concatenate([cache_v[li, pages].reshape(past, MOBA_HEADS, MOBA_HD), vb_i], 0)
        bk, bv, km = moba_blocks(k_all[None], v_all[None])
        ob = moba_attend(qb_i[None], qpos, bk, bv, km)[0]
        return oa, ob

    oa, ob = lax.map(one, (qa, ckv, kr, qb, kb, vb, page_table))
    mix = jnp.concatenate([oa.reshape(n, t, -1), ob.reshape(n, t, -1)], -1)
    return mix @ w_out, (ckv, kr, kb, vb)


def odd_split(z):
    n, l = z.shape[:2]
    nk, nv = GLA_HEADS * GLA_DK, GLA_HEADS * GLA_DV
    q = z[..., :nk].reshape(n, l, GLA_HEADS, GLA_DK) * GLA_DK ** -0.5
    k = z[..., nk:2 * nk].reshape(n, l, GLA_HEADS, GLA_DK)
    v = z[..., 2 * nk:2 * nk + nv].reshape(n, l, GLA_HEADS, GLA_DV)
    r = z[..., 2 * nk + nv:2 * nk + 2 * nv]
    glr = z[..., 2 * nk + 2 * nv:2 * nk + 2 * nv + GLA_GATE_RANK]
    u = z[..., 2 * nk + 2 * nv + GLA_GATE_RANK:]
    return q, k, v, r, glr, u


def gla_gate(glr, w_g, b_g):
    n, l = glr.shape[:2]
    z = (glr @ w_g + b_g).astype(jnp.float32)
    return (jax.nn.log_sigmoid(z) / GLA_TAU).reshape(n, l, GLA_HEADS, GLA_DK)


def gla_output(o, r, g_o):
    n, l = r.shape[:2]
    return rms_norm(o, g_o).astype(r.dtype).reshape(n, l, -1) * jax.nn.silu(r)


def odd_prompt(h, w_in, w_g, b_g, g_o, w_pool, pool_scale, w_out):
    q, k, v, r, glr, u = odd_split(h @ w_in)
    s_fin, o = gla_prompt(q, k, v, gla_gate(glr, w_g, b_g))
    u_ext = jnp.pad(u, ((0, 0), (POOL_HIST, 0), (0, 0)))
    mix = jnp.concatenate([gla_output(o, r, g_o), pool_mix(u_ext, 0, w_pool, pool_scale)], -1)
    return mix @ w_out, (s_fin.astype(h.dtype), u[:, -POOL_HIST:])


def odd_sample(h, past, s_prev, u_prev, w_in, w_g, b_g, g_o, w_pool, pool_scale, w_out):
    q, k, v, r, glr, u = odd_split(h @ w_in)
    s_new, o = gla_chunk(s_prev.astype(jnp.float32), q, k, v, gla_gate(glr, w_g, b_g))
    u_ext = jnp.concatenate([u_prev.astype(u.dtype), u], 1)
    mix = jnp.concatenate([gla_output(o, r, g_o), pool_mix(u_ext, past, w_pool, pool_scale)], -1)
    return mix @ w_out, (s_new.astype(s_prev.dtype), u_ext[:, -POOL_HIST:])


def mem_kv(mem, g_m, w_k, w_v, g_k):
    mn = rms_norm(mem, g_m)
    k = rms_norm(jnp.einsum('nmd,dhe->nmhe', mn, w_k), g_k)
    v = jnp.einsum('nmd,dhe->nmhe', mn, w_v)
    return k, v


def mem_attend(h, k, v, w_q, g_q, w_o):
    q = rms_norm(jnp.einsum('nld,dhe->nlhe', h, w_q), g_q)
    sc = jnp.einsum('nlhe,nmhe->nhlm', q, k).astype(jnp.float32) * MEM_HD ** -0.5
    p = jax.nn.softmax(sc, axis=-1).astype(v.dtype)
    return jnp.einsum('nlhe,hed->nld', jnp.einsum('nhlm,nmhe->nlhe', p, v), w_o)


def ffn(h, w1, w2):
    return jnp.square(jax.nn.relu(h @ w1)) @ w2


def setup_inputs(seed: int = 0) -> dict:
    key = jax.random.key(seed)
    keys = list(jax.random.split(key, 64))
    f32 = jnp.float32

    def nrm(shape, scale=1.0):
        return jax.random.normal(keys.pop(), shape, f32) * scale

    def gain(shape):
        return 1.0 + 0.1 * jax.random.normal(keys.pop(), shape, f32)

    d = D_MODEL
    n_pages = PAST_LEN // PAGE_SIZE
    n_used = DEC_BATCH * n_pages
    n_pool = n_used + n_used // 4
    page_table = jax.random.permutation(keys.pop(), n_pool)[:n_used].reshape(DEC_BATCH, n_pages).astype(jnp.int32)
    return {
        'x_prompt': nrm((BATCH, SEQ, d)),
        'x_sample': nrm((DEC_BATCH, DEC_SEQ, d)),
        'cache_mla_ckv': nrm((N_EVEN, n_pool, PAGE_SIZE, MLA_KV_LORA)),
        'cache_mla_krope': nrm((N_EVEN, n_pool, PAGE_SIZE, MLA_ROPE)),
        'cache_moba_k': nrm((N_EVEN, n_pool, PAGE_SIZE, MOBA_HEADS, MOBA_HD)),
        'cache_moba_v': nrm((N_EVEN, n_pool, PAGE_SIZE, MOBA_HEADS, MOBA_HD)),
        'state_gla': nrm((N_ODD, DEC_BATCH, GLA_HEADS, GLA_DK, GLA_DV)),
        'state_pool': nrm((N_ODD, DEC_BATCH, POOL_HIST, POOL_DIM)),
        'cache_mem_k': nrm((DEPTH, DEC_BATCH, MEM_LEN, MEM_HEADS, MEM_HD)),
        'cache_mem_v': nrm((DEPTH, DEC_BATCH, MEM_LEN, MEM_HEADS, MEM_HD)),
        'page_table': page_table,
        'mem_prompt': nrm((BATCH, MEM_LEN, d)),
        'g_mix': gain((DEPTH, d)),
        'g_memx': gain((DEPTH, d)),
        'g_memm': gain((DEPTH, d)),
        'w_mq': nrm((DEPTH, d, MEM_HEADS, MEM_HD), d ** -0.5),
        'w_mk': nrm((DEPTH, d, MEM_HEADS, MEM_HD), d ** -0.5),
        'w_mv': nrm((DEPTH, d, MEM_HEADS, MEM_HD), d ** -0.5),
        'g_mq': gain((DEPTH, MEM_HD)),
        'g_mk': gain((DEPTH, MEM_HD)),
        'w_mo': nrm((DEPTH, MEM_HEADS, MEM_HD, d), (MEM_HEADS * MEM_HD) ** -0.5),
        'g_ffn': gain((DEPTH, d)),
        'w_ff1': nrm((DEPTH, d, D_FF), d ** -0.5),
        'w_ff2': nrm((DEPTH, D_FF, d), D_FF ** -0.5),
        'ev_w_in': nrm((N_EVEN, d, EVEN_IN), d ** -0.5),
        'ev_g_cq': gain((N_EVEN, MLA_Q_LORA)),
        'ev_g_ckv': gain((N_EVEN, MLA_KV_LORA)),
        'ev_w_uq': nrm((N_EVEN, MLA_Q_LORA, MLA_HEADS, MLA_QK), MLA_Q_LORA ** -0.5),
        'ev_w_uk': nrm((N_EVEN, MLA_KV_LORA, MLA_HEADS, MLA_NOPE), MLA_KV_LORA ** -0.5),
        'ev_w_uv': nrm((N_EVEN, MLA_KV_LORA, MLA_HEADS, MLA_V), MLA_KV_LORA ** -0.5),
        'ev_g_mla_q': gain((N_EVEN, MLA_QK)),
        'ev_g_mla_k': gain((N_EVEN, MLA_QK)),
        'ev_g_moba_q': gain((N_EVEN, MOBA_HD)),
        'ev_g_moba_k': gain((N_EVEN, MOBA_HD)),
        'ev_w_out': nrm((N_EVEN, EVEN_MIX, d), EVEN_MIX ** -0.5),
        'od_w_in': nrm((N_ODD, d, ODD_IN), d ** -0.5),
        'od_w_gate': nrm((N_ODD, GLA_GATE_RANK, GLA_HEADS * GLA_DK), GLA_GATE_RANK ** -0.5),
        'od_b_gate': nrm((N_ODD, GLA_HEADS * GLA_DK), 0.1),
        'od_g_gla': gain((N_ODD, GLA_DV)),
        'od_w_pool': nrm((N_ODD, len(POOL_WINDOWS), POOL_GDIM, POOL_GDIM), POOL_GDIM ** -0.5),
        'od_pool_scale': gain((N_ODD, POOL_DIM)),
        'od_w_out': nrm((N_ODD, ODD_MIX, d), ODD_MIX ** -0.5),
    }


def reference(x_prompt, x_sample, cache_mla_ckv, cache_mla_krope, cache_moba_k, cache_moba_v,
              state_gla, state_pool, cache_mem_k, cache_mem_v, page_table, mem_prompt,
              g_mix, g_memx, g_memm, w_mq, w_mk, w_mv, g_mq, g_mk, w_mo, g_ffn, w_ff1, w_ff2,
              ev_w_in, ev_g_cq, ev_g_ckv, ev_w_uq, ev_w_uk, ev_w_uv, ev_g_mla_q, ev_g_mla_k,
              ev_g_moba_q, ev_g_moba_k, ev_w_out,
              od_w_in, od_w_gate, od_b_gate, od_g_gla, od_w_pool, od_pool_scale, od_w_out):
    past = page_table.shape[1] * PAGE_SIZE
    yp, ys = x_prompt, x_sample
    p_ckv, p_kr, p_k, p_v, s_ckv, s_kr, s_k, s_v = [], [], [], [], [], [], [], []
    p_gla, p_pool, s_gla, s_pool, p_mk, p_mv = [], [], [], [], [], []
    for l in range(DEPTH):
        hp = rms_norm(yp, g_mix[l])
        hs = rms_norm(ys, g_mix[l])
        i = l // 2
        if l % 2 == 0:
            ew = (ev_w_in[i], ev_g_cq[i], ev_g_ckv[i], ev_w_uq[i], ev_w_uk[i], ev_w_uv[i],
                  ev_g_mla_q[i], ev_g_mla_k[i], ev_g_moba_q[i], ev_g_moba_k[i], ev_w_out[i])
            dp, (a0, a1, a2, a3) = even_prompt(hp, *ew)
            ds, (b0, b1, b2, b3) = even_sample(hs, i, page_table, cache_mla_ckv, cache_mla_krope,
                                               cache_moba_k, cache_moba_v, *ew)
            p_ckv.append(a0); p_kr.append(a1); p_k.append(a2); p_v.append(a3)
            s_ckv.append(b0); s_kr.append(b1); s_k.append(b2); s_v.append(b3)
        else:
            ow = (od_w_in[i], od_w_gate[i], od_b_gate[i], od_g_gla[i], od_w_pool[i],
                  od_pool_scale[i], od_w_out[i])
            dp, (a0, a1) = odd_prompt(hp, *ow)
            ds, (b0, b1) = odd_sample(hs, past, state_gla[i], state_pool[i], *ow)
            p_gla.append(a0); p_pool.append(a1); s_gla.append(b0); s_pool.append(b1)
        yp = yp + dp
        ys = ys + ds
        mk, mv = mem_kv(mem_prompt, g_memm[l], w_mk[l], w_mv[l], g_mk[l])
        p_mk.append(mk); p_mv.append(mv)
        yp = yp + mem_attend(rms_norm(yp, g_memx[l]), mk, mv, w_mq[l], g_mq[l], w_mo[l])
        ys = ys + mem_attend(rms_norm(ys, g_memx[l]), cache_mem_k[l], cache_mem_v[l],
                             w_mq[l], g_mq[l], w_mo[l])
        yp = yp + ffn(rms_norm(yp, g_ffn[l]), w_ff1[l], w_ff2[l])
        ys = ys + ffn(rms_norm(ys, g_ffn[l]), w_ff1[l], w_ff2[l])
    return (yp, ys,
            jnp.stack(p_ckv), jnp.stack(p_kr), jnp.stack(p_k), jnp.stack(p_v),
            jnp.stack(p_gla), jnp.stack(p_pool), jnp.stack(p_mk), jnp.stack(p_mv),
            jnp.stack(s_ckv), jnp.stack(s_kr), jnp.stack(s_k), jnp.stack(s_v),
            jnp.stack(s_gla), jnp.stack(s_pool))
```

```python
import functools
import math

import jax
import jax.numpy as jnp
import numpy as np
from jax import lax
from jax.experimental import pallas as pl
from jax.experimental.pallas import tpu as pltpu

F32 = jnp.float32
BF16 = jnp.bfloat16
NEG_INF = float("-inf")

EPS = 1e-6
D_MODEL = 1024
PAGE_SIZE = 128
MLA_HEADS = 8
MLA_NOPE = 64
MLA_ROPE = 32
MLA_QK = MLA_NOPE + MLA_ROPE
MLA_V = 64
MLA_Q_LORA = 384
MLA_KV_LORA = 256
MLA_THETA = 10000.0
MLA_PAIR = 256
MOBA_HEADS = 8
MOBA_HD = 64
MOBA_ROT = MOBA_HD // 4
MOBA_BLOCK = 256
MOBA_TOPK = 3
ROPE_THETA = 500000.0
GLA_HEADS = 4
GLA_DK = 128
GLA_DV = 192
GLA_DVP = 256
GLA_GATE_RANK = 16
GLA_TAU = 16.0
GLA_CHUNK = 64
POOL_WINDOWS = (2, 4, 8, 16)
POOL_GDIM = 64
POOL_DIM = POOL_GDIM * len(POOL_WINDOWS)
POOL_HIST = max(POOL_WINDOWS) - 1
MEM_HEADS = 4
MEM_HD = 128
LANE = 128
VMEM_LIMIT = 56 * 1024 * 1024


def _rms(x, g):
    return x * lax.rsqrt(jnp.mean(x * x, axis=-1, keepdims=True) + EPS) * g


def _dot(a, b):
    return jnp.dot(a.astype(BF16), b.astype(BF16), preferred_element_type=F32)


def _dot_nt(a, b):
    return lax.dot_general(a.astype(BF16), b.astype(BF16), (((1,), (1,)), ((), ())),
                           preferred_element_type=F32)


def _dot_tn(a, b):
    return lax.dot_general(a.astype(BF16), b.astype(BF16), (((0,), (0,)), ((), ())),
                           preferred_element_type=F32)


def _split(a):
    hi = a.astype(BF16)
    lo = (a - hi.astype(F32)).astype(BF16)
    return hi, lo


def _dot_hl(a, sel):
    hi, lo = _split(a)
    return (jnp.dot(hi, sel, preferred_element_type=F32)
            + jnp.dot(lo, sel, preferred_element_type=F32))


def _dot_hl3(a, sel):
    hi = a.astype(BF16)
    r = a - hi.astype(F32)
    mid = r.astype(BF16)
    lo = (r - mid.astype(F32)).astype(BF16)
    return (jnp.dot(hi, sel, preferred_element_type=F32)
            + jnp.dot(mid, sel, preferred_element_type=F32)
            + jnp.dot(lo, sel, preferred_element_type=F32))


def _cparams(*sem):
    return pltpu.CompilerParams(dimension_semantics=sem, vmem_limit_bytes=VMEM_LIMIT)


def _const_spec(shape):
    nd = len(shape)
    return pl.BlockSpec(shape, lambda *_: (0,) * nd)


def _mla_cols():
    cols = np.zeros((MLA_HEADS, MLA_QK), np.int32)
    for h in range(MLA_HEADS):
        base = MLA_PAIR * (h // 2)
        cols[h, :MLA_NOPE] = base + MLA_NOPE * (h % 2) + np.arange(MLA_NOPE)
        cols[h, MLA_NOPE:] = base + 2 * MLA_NOPE + MLA_ROPE * (h % 2) + np.arange(MLA_ROPE)
    return cols


def _rope_tables(pos, half, theta):
    inv = jnp.exp(jnp.arange(half, dtype=F32) * (-math.log(theta) / half))
    ang = pos.astype(F32)[:, None] * inv[None, :]
    return jnp.cos(ang), jnp.sin(ang)


def _mla_pair_tables(pos):
    c, s = _rope_tables(pos, MLA_ROPE // 2, MLA_THETA)
    p = pos.shape[0]
    one = jnp.ones((p, 2 * MLA_NOPE), F32)
    pad1 = jnp.ones((p, MLA_PAIR - 2 * MLA_NOPE - 2 * MLA_ROPE), F32)
    ct = jnp.concatenate([one, c, c, c, c, pad1], -1)
    st = jnp.concatenate([0 * one, s, s, s, s, 0 * pad1], -1)
    return ct, st


def _moba_pair_tables(pos):
    c, s = _rope_tables(pos, MOBA_ROT // 2, ROPE_THETA)
    p = pos.shape[0]
    one = jnp.ones((p, MOBA_HD - MOBA_ROT), F32)
    ch = jnp.concatenate([c, c, one], -1)
    sh = jnp.concatenate([s, s, 0 * one], -1)
    return jnp.concatenate([ch, ch], -1), jnp.concatenate([sh, sh], -1)


def _even_weights(w_in, g_cq, g_ckv, w_uq, w_uk, w_uv, g_aq, g_ak, g_bq, g_bk):
    i0 = MLA_Q_LORA + MLA_KV_LORA
    i1 = i0 + MLA_ROPE
    idx = np.concatenate([np.arange(0, i0), np.arange(i1, w_in.shape[1]), np.arange(i0, i1)])
    win = jnp.pad(w_in[:, idx], ((0, 0), (0, LANE - MLA_ROPE))).astype(BF16)
    cols = _mla_cols()
    width = MLA_PAIR * MLA_HEADS // 2
    wq = jnp.zeros((MLA_Q_LORA, width), F32).at[:, cols.reshape(-1)].set(
        w_uq.reshape(MLA_Q_LORA, -1)).astype(BF16)
    wk = jnp.zeros((MLA_KV_LORA, width), F32).at[:, cols[:, :MLA_NOPE].reshape(-1)].set(
        w_uk.reshape(MLA_KV_LORA, -1)).astype(BF16)
    wv = w_uv.reshape(MLA_KV_LORA, -1).astype(BF16)
    pk = np.zeros((MLA_ROPE, width), np.float32)
    e = np.zeros((width, LANE), np.float32)
    for h in range(MLA_HEADS):
        pk[np.arange(MLA_ROPE), cols[h, MLA_NOPE:]] = 1.0
        e[cols[h], h] = 1.0
    gaq = jnp.zeros((1, width), F32).at[0, cols.reshape(-1)].set(jnp.tile(g_aq, MLA_HEADS))
    gak = jnp.zeros((1, width), F32).at[0, cols.reshape(-1)].set(jnp.tile(g_ak, MLA_HEADS))
    b64 = np.kron(np.eye(MOBA_HEADS, dtype=np.float32), np.ones((MOBA_HD, MOBA_HD), np.float32))
    return dict(
        win=win, gcq=g_cq[None], gckv=g_ckv[None], wq=wq, wk=wk, wv=wv,
        pk=jnp.asarray(pk, BF16), e=jnp.asarray(e, BF16), et=jnp.asarray(e.T, BF16),
        gaq=gaq, gak=gak, b64=jnp.asarray(b64, BF16),
        gbq=jnp.tile(g_bq, MOBA_HEADS)[None], gbk=jnp.tile(g_bk, MOBA_HEADS)[None])


def _even_proj_body(x_ref, gmix_ref, win_ref, gcq_ref, gckv_ref, wq_ref, wk_ref, wv_ref, pk_ref,
                    e_ref, et_ref, gaq_ref, gak_ref, ca_ref, sa_ref, b64_ref, gbq_ref, gbk_ref,
                    cb_ref, sb_ref,
                    ckv_ref, kr_ref, qa_ref, ka_ref, va_ref, qb_ref, kb_ref, vb_ref, km_ref):
    xn = _rms(x_ref[...], gmix_ref[...])
    z = _dot(xn, win_ref[...])
    o_ckv = MLA_Q_LORA
    o_qb = o_ckv + MLA_KV_LORA
    hb = MOBA_HEADS * MOBA_HD
    cqn = _rms(z[:, :o_ckv], gcq_ref[...])
    ckv = _rms(z[:, o_ckv:o_qb], gckv_ref[...])
    qb = z[:, o_qb:o_qb + hb]
    kb = z[:, o_qb + hb:o_qb + 2 * hb]
    vb = z[:, o_qb + 2 * hb:o_qb + 3 * hb]
    kr = z[:, o_qb + 3 * hb:o_qb + 3 * hb + MLA_ROPE]
    ckv_ref[...] = ckv
    kr_ref[...] = kr
    vb_ref[...] = vb

    npair = MLA_HEADS // 2
    ca = jnp.concatenate([ca_ref[...]] * npair, axis=-1)
    sa = jnp.concatenate([sa_ref[...]] * npair, axis=-1)
    width = MLA_PAIR * npair
    lane = lax.broadcasted_iota(jnp.int32, (1, width), 1)
    in_pair = lane % MLA_PAIR
    first_half = ((in_pair >= 2 * MLA_NOPE) & (in_pair < 2 * MLA_NOPE + 2 * MLA_ROPE)
                  & (lane % MLA_ROPE < MLA_ROPE // 2))

    def norm_rope_a(t, g):
        ss = _dot_hl(t * t, e_ref[...])
        rs = lax.rsqrt(ss * (1.0 / MLA_QK) + EPS)
        tn = t * _dot_hl(rs, et_ref[...]) * g
        sw = jnp.where(first_half, -pltpu.roll(tn, width - MLA_ROPE // 2, 1),
                       pltpu.roll(tn, MLA_ROPE // 2, 1))
        return tn * ca + sw * sa

    qa = norm_rope_a(_dot(cqn, wq_ref[...]), gaq_ref[...]) * (MLA_QK ** -0.5)
    ka = norm_rope_a(_dot(ckv, wk_ref[...]) + _dot_hl(kr, pk_ref[...]), gak_ref[...])
    qa_ref[...] = qa.astype(BF16)
    ka_ref[...] = ka.astype(BF16)
    va_ref[...] = _dot(ckv, wv_ref[...]).astype(BF16)

    cb = jnp.concatenate([cb_ref[...]] * (MOBA_HEADS // 2), axis=-1)
    sb = jnp.concatenate([sb_ref[...]] * (MOBA_HEADS // 2), axis=-1)
    lane_b = lax.broadcasted_iota(jnp.int32, (1, hb), 1)
    first_half_b = lane_b % MOBA_HD < MOBA_ROT // 2

    def norm_rope_b(t, g):
        ss = _dot_hl(t * t, b64_ref[...])
        tn = t * lax.rsqrt(ss * (1.0 / MOBA_HD) + EPS) * g
        sw = jnp.where(first_half_b, -pltpu.roll(tn, hb - MOBA_ROT // 2, 1),
                       pltpu.roll(tn, MOBA_ROT // 2, 1))
        return tn * cb + sw * sb

    qb_ref[...] = norm_rope_b(qb, gbq_ref[...])
    kbn = norm_rope_b(kb, gbk_ref[...])
    kb_ref[...] = kbn
    km_ref[0] = jnp.mean(kbn, axis=0, keepdims=True)


def _even_proj(x, g_mix, ew, pos, tm):
    m, d = x.shape
    nt = m // tm
    ca, sa = _mla_pair_tables(pos)
    cb, sb = _moba_pair_tables(pos)
    npt = pos.shape[0] // tm
    row = lambda w: pl.BlockSpec((tm, w), lambda i: (i, 0))
    tab = lambda w: pl.BlockSpec((tm, w), lambda i: (i % npt, 0))
    hb = MOBA_HEADS * MOBA_HD
    wa = MLA_PAIR * MLA_HEADS // 2
    consts = [g_mix[None], ew["win"], ew["gcq"], ew["gckv"], ew["wq"], ew["wk"], ew["wv"], ew["pk"],
              ew["e"], ew["et"], ew["gaq"], ew["gak"]]
    consts2 = [ew["b64"], ew["gbq"], ew["gbk"]]
    in_specs = ([row(d)] + [_const_spec(c.shape) for c in consts] + [tab(MLA_PAIR), tab(MLA_PAIR)]
                + [_const_spec(c.shape) for c in consts2] + [tab(LANE), tab(LANE)])
    out_shape = [
        jax.ShapeDtypeStruct((m, MLA_KV_LORA), F32), jax.ShapeDtypeStruct((m, MLA_ROPE), F32),
        jax.ShapeDtypeStruct((m, wa), BF16), jax.ShapeDtypeStruct((m, wa), BF16),
        jax.ShapeDtypeStruct((m, MLA_HEADS * MLA_V), BF16),
        jax.ShapeDtypeStruct((m, hb), F32), jax.ShapeDtypeStruct((m, hb), F32),
        jax.ShapeDtypeStruct((m, hb), F32), jax.ShapeDtypeStruct((nt, 1, hb), F32)]
    out_specs = [row(MLA_KV_LORA), row(MLA_ROPE), row(wa), row(wa), row(MLA_HEADS * MLA_V),
                 row(hb), row(hb), row(hb), pl.BlockSpec((1, 1, hb), lambda i: (i, 0, 0))]
    return pl.pallas_call(
        _even_proj_body, grid=(nt,), in_specs=in_specs, out_specs=out_specs, out_shape=out_shape,
        compiler_params=_cparams("parallel"), name="even_proj",
    )(x, *consts, ca, sa, *consts2, cb, sb)


def _mla_head_mask(e):
    lane = lax.broadcasted_iota(jnp.int32, (1, MLA_PAIR), 1)
    nope = (lane >= MLA_NOPE * e) & (lane < MLA_NOPE * (e + 1))
    r0 = 2 * MLA_NOPE + MLA_ROPE * e
    return nope | ((lane >= r0) & (lane < r0 + MLA_ROPE))


def _mla_attn_body(q_ref, k_ref, v_ref, o_ref, m_sc, l_sc, acc_sc, *, tq):
    qi = pl.program_id(2)
    m_sc[...] = jnp.full(m_sc.shape, NEG_INF, F32)
    l_sc[...] = jnp.zeros(l_sc.shape, F32)
    acc_sc[...] = jnp.zeros(acc_sc.shape, F32)
    q = q_ref[0]
    qs = [jnp.where(_mla_head_mask(e), q, jnp.zeros_like(q)) for e in (0, 1)]
    row = lax.broadcasted_iota(jnp.int32, (tq, tq), 0)
    col = lax.broadcasted_iota(jnp.int32, (tq, tq), 1)

    def step(ki, carry):
        start = pl.multiple_of(ki * tq, tq)
        k = k_ref[0, pl.ds(start, tq), :]
        v = v_ref[0, pl.ds(start, tq), :]
        visible = (col <= row) | (ki < qi)
        for e in (0, 1):
            s = jnp.where(visible, _dot_nt(qs[e], k), NEG_INF)
            m_prev = m_sc[e]
            m_new = jnp.maximum(m_prev, jnp.max(s, axis=-1, keepdims=True))
            alpha = jnp.exp(m_prev - m_new)
            p = jnp.exp(s - m_new)
            l_sc[e] = alpha * l_sc[e] + jnp.sum(p, axis=-1, keepdims=True)
            acc_sc[e] = alpha * acc_sc[e] + _dot(p, v)
            m_sc[e] = m_new
        return carry

    lax.fori_loop(0, qi + 1, step, 0)
    lane_o = lax.broadcasted_iota(jnp.int32, (1, 2 * MLA_V), 1)
    o_ref[0] = jnp.where(lane_o < MLA_V, acc_sc[0] / l_sc[0], acc_sc[1] / l_sc[1])


def _mla_attn(qa, ka, va, tq=512):
    n, s, _ = qa.shape
    npair = MLA_HEADS // 2
    return pl.pallas_call(
        functools.partial(_mla_attn_body, tq=tq),
        grid=(n, npair, s // tq),
        in_specs=[pl.BlockSpec((1, tq, MLA_PAIR), lambda b, p, i: (b, i, p)),
                  pl.BlockSpec((1, s, MLA_PAIR), lambda b, p, i: (b, 0, p)),
                  pl.BlockSpec((1, s, 2 * MLA_V), lambda b, p, i: (b, 0, p))],
        out_specs=pl.BlockSpec((1, tq, 2 * MLA_V), lambda b, p, i: (b, i, p)),
        out_shape=jax.ShapeDtypeStruct((n, s, MLA_HEADS * MLA_V), F32),
        scratch_shapes=[pltpu.VMEM((2, tq, 1), F32), pltpu.VMEM((2, tq, 1), F32),
                        pltpu.VMEM((2, tq, 2 * MLA_V), F32)],
        compiler_params=_cparams("parallel", "parallel", "arbitrary"), name="mla_attn",
    )(qa, ka, va)


def _top_blocks(gate, eligible, axis):
    idx = lax.broadcasted_iota(jnp.int32, gate.shape, axis)
    big = jnp.int32(1 << 20)
    g = jnp.where(eligible, gate, NEG_INF)
    sel = jnp.zeros(gate.shape, jnp.bool_)
    for _ in range(MOBA_TOPK):
        mx = jnp.max(g, axis=axis, keepdims=True)
        cand = (g == mx) & (mx > NEG_INF)
        first = jnp.min(jnp.where(cand, idx, big), axis=axis, keepdims=True)
        pick = idx == first
        sel = sel | pick
        g = jnp.where(pick, NEG_INF, g)
    return sel


def _moba_attn_body(q_ref, k_ref, v_ref, km_ref, o_ref, m_sc, l_sc, acc_sc, sel_sc, *, tq):
    qi = pl.program_id(2)
    q = q_ref[0]
    lane = lax.broadcasted_iota(jnp.int32, (1, 2 * MOBA_HD), 1)
    nblk = km_ref.shape[1]
    blk = lax.broadcasted_iota(jnp.int32, (tq, nblk), 1)
    km_hi, km_lo = _split(km_ref[0])
    qs = []
    for e in (0, 1):
        qe = jnp.where((lane >= MOBA_HD * e) & (lane < MOBA_HD * (e + 1)), q, 0.0)
        qs.append(qe.astype(BF16))
        q_hi, q_lo = _split(qe)
        nt = (((1,), (1,)), ((), ()))
        gate = (lax.dot_general(q_hi, km_hi, nt, preferred_element_type=F32)
                + lax.dot_general(q_hi, km_lo, nt, preferred_element_type=F32)
                + lax.dot_general(q_lo, km_hi, nt, preferred_element_type=F32))
        sel_sc[e] = _top_blocks(gate, blk < qi, 1).astype(F32)
    m_sc[...] = jnp.full(m_sc.shape, NEG_INF, F32)
    l_sc[...] = jnp.zeros(l_sc.shape, F32)
    acc_sc[...] = jnp.zeros(acc_sc.shape, F32)
    causal = (lax.broadcasted_iota(jnp.int32, (tq, tq), 1)
              <= lax.broadcasted_iota(jnp.int32, (tq, tq), 0)).astype(F32)

    def step(t, carry):
        j = qi - t
        start = pl.multiple_of(j * tq, tq)
        k = k_ref[0, pl.ds(start, tq), :].astype(BF16)
        v = v_ref[0, pl.ds(start, tq), :].astype(BF16)
        for e in (0, 1):
            picked = jnp.sum(jnp.where(blk == j, sel_sc[e], 0.0), axis=1, keepdims=True)
            ok = jnp.where(t == 0, causal, picked) > 0.5
            s = jnp.where(ok, _dot_nt(qs[e], k) * (MOBA_HD ** -0.5), NEG_INF)
            m_prev = m_sc[e]
            m_new = jnp.maximum(m_prev, jnp.max(s, axis=-1, keepdims=True))
            alpha = jnp.exp(m_prev - m_new)
            p = jnp.exp(s - m_new)
            l_sc[e] = alpha * l_sc[e] + jnp.sum(p, axis=-1, keepdims=True)
            acc_sc[e] = alpha * acc_sc[e] + _dot(p, v)
            m_sc[e] = m_new
        return carry

    lax.fori_loop(0, qi + 1, step, 0)
    o_ref[0] = jnp.where(lane < MOBA_HD, acc_sc[0] / l_sc[0], acc_sc[1] / l_sc[1])


def _moba_attn(qb, kb, vb, km):
    n, s, hb = qb.shape
    tq = MOBA_BLOCK
    nblk = s // tq
    return pl.pallas_call(
        functools.partial(_moba_attn_body, tq=tq),
        grid=(n, hb // LANE, nblk),
        in_specs=[pl.BlockSpec((1, tq, LANE), lambda b, p, i: (b, i, p)),
                  pl.BlockSpec((1, s, LANE), lambda b, p, i: (b, 0, p)),
                  pl.BlockSpec((1, s, LANE), lambda b, p, i: (b, 0, p)),
                  pl.BlockSpec((1, nblk, LANE), lambda b, p, i: (b, 0, p))],
        out_specs=pl.BlockSpec((1, tq, LANE), lambda b, p, i: (b, i, p)),
        out_shape=jax.ShapeDtypeStruct((n, s, hb), F32),
        scratch_shapes=[pltpu.VMEM((2, tq, 1), F32), pltpu.VMEM((2, tq, 1), F32),
                        pltpu.VMEM((2, tq, LANE), F32), pltpu.VMEM((2, tq, nblk), F32)],
        compiler_params=_cparams("parallel", "parallel", "arbitrary"), name="moba_attn",
    )(qb, kb, vb, km)


def _mm2_res_body(a_ref, b_ref, wa_ref, wb_ref, r_ref, o_ref):
    o_ref[...] = r_ref[...] + _dot(a_ref[...], wa_ref[...]) + _dot(b_ref[...], wb_ref[...])


def _mm2_res(a, b, wa, wb, res, tm):
    m, d = res.shape
    row = lambda w: pl.BlockSpec((tm, w), lambda i: (i, 0))
    return pl.pallas_call(
        _mm2_res_body, grid=(m // tm,),
        in_specs=[row(a.shape[1]), row(b.shape[1]), _const_spec(wa.shape), _const_spec(wb.shape), row(d)],
        out_specs=row(d), out_shape=jax.ShapeDtypeStruct((m, d), F32),
        compiler_params=_cparams("parallel"), name="mix_out",
    )(a, b, wa, wb, res)


def _mem_kv_body(mem_ref, gm_ref, wk_ref, wv_ref, gk_ref, k_ref, v_ref):
    mn = _rms(mem_ref[0], gm_ref[...])
    k = _dot(mn, wk_ref[...])
    for h in range(MEM_HEADS):
        sl = slice(h * MEM_HD, (h + 1) * MEM_HD)
        k_ref[0, :, sl] = _rms(k[:, sl], gk_ref[...])
    v_ref[0] = _dot(mn, wv_ref[...])


def _mem_kv(mem, g_m, w_k, w_v, g_k):
    n, ml, d = mem.shape
    hd = MEM_HEADS * MEM_HD
    blk = lambda w: pl.BlockSpec((1, ml, w), lambda b: (b, 0, 0))
    return pl.pallas_call(
        _mem_kv_body, grid=(n,),
        in_specs=[blk(d), _const_spec((1, d)), _const_spec((d, hd)), _const_spec((d, hd)),
                  _const_spec((1, MEM_HD))],
        out_specs=[blk(hd), blk(hd)],
        out_shape=[jax.ShapeDtypeStruct((n, ml, hd), F32)] * 2,
        compiler_params=_cparams("parallel"), name="mem_kv",
    )(mem, g_m[None], w_k.reshape(d, hd).astype(BF16), w_v.reshape(d, hd).astype(BF16), g_k[None])


def _mem_attn_body(x_ref, g_ref, wq_ref, gq_ref, k_ref, v_ref, wo_ref, o_ref):
    x = x_ref[0]
    q = _dot(_rms(x, g_ref[...]), wq_ref[...])
    k = k_ref[0].astype(BF16)
    v = v_ref[0].astype(BF16)
    outs = []
    for h in range(MEM_HEADS):
        sl = slice(h * MEM_HD, (h + 1) * MEM_HD)
        qh = _rms(q[:, sl], gq_ref[...])
        s = _dot_nt(qh, k[:, sl]) * (MEM_HD ** -0.5)
        p = jnp.exp(s - jnp.max(s, axis=-1, keepdims=True))
        p = p / jnp.sum(p, axis=-1, keepdims=True)
        outs.append(_dot(p, v[:, sl]))
    o_ref[0] = x + _dot(jnp.concatenate(outs, axis=-1), wo_ref[...])


def _mem_attn(x, g, w_q, g_q, k, v, w_o, tm):
    n, s, d = x.shape
    ml, hd = k.shape[1:]
    return pl.pallas_call(
        _mem_attn_body, grid=(n, s // tm),
        in_specs=[pl.BlockSpec((1, tm, d), lambda b, i: (b, i, 0)), _const_spec((1, d)),
                  _const_spec((d, hd)), _const_spec((1, MEM_HD)),
                  pl.BlockSpec((1, ml, hd), lambda b, i: (b, 0, 0)),
                  pl.BlockSpec((1, ml, hd), lambda b, i: (b, 0, 0)), _const_spec((hd, d))],
        out_specs=pl.BlockSpec((1, tm, d), lambda b, i: (b, i, 0)),
        out_shape=jax.ShapeDtypeStruct((n, s, d), F32),
        compiler_params=_cparams("parallel", "parallel"), name="mem_attn",
    )(x, g[None], w_q.reshape(d, hd).astype(BF16), g_q[None], k, v, w_o.reshape(hd, d).astype(BF16))


def _mem_attn_sample_body(x_ref, g_ref, wq_ref, gq_ref, k_ref, v_ref, wo_ref, o_ref, att_sc, *, bs):
    x = x_ref[...]
    q = _dot(_rms(x, g_ref[...]), wq_ref[...])
    hd = MEM_HEADS * MEM_HD
    qn = jnp.concatenate(
        [_rms(q[:, h * MEM_HD:(h + 1) * MEM_HD], gq_ref[...]) for h in range(MEM_HEADS)], axis=-1)
    rows = 8
    hrow = lax.broadcasted_iota(jnp.int32, (rows, hd), 0)
    hlane = lax.broadcasted_iota(jnp.int32, (rows, hd), 1) // MEM_HD
    own = hrow == hlane
    for b in range(bs):
        qbd = jnp.where(own, jnp.broadcast_to(qn[b:b + 1], (rows, hd)), 0.0)
        s = _dot_nt(qbd, k_ref[b]) * (MEM_HD ** -0.5)
        p = jnp.exp(s - jnp.max(s, axis=-1, keepdims=True))
        p = p / jnp.sum(p, axis=-1, keepdims=True)
        o = _dot(p, v_ref[b])
        att_sc[b:b + 1, :] = jnp.sum(jnp.where(own, o, 0.0), axis=0, keepdims=True)
    o_ref[...] = x + _dot(att_sc[...], wo_ref[...])


def _mem_attn_sample(x, g, w_q, g_q, k, v, w_o, bs=8):
    b, d = x.shape
    ml, hd = k.shape[1:]
    return pl.pallas_call(
        functools.partial(_mem_attn_sample_body, bs=bs), grid=(b // bs,),
        in_specs=[pl.BlockSpec((bs, d), lambda i: (i, 0)), _const_spec((1, d)),
                  _const_spec((d, hd)), _const_spec((1, MEM_HD)),
                  pl.BlockSpec((bs, ml, hd), lambda i: (i, 0, 0)),
                  pl.BlockSpec((bs, ml, hd), lambda i: (i, 0, 0)), _const_spec((hd, d))],
        out_specs=pl.BlockSpec((bs, d), lambda i: (i, 0)),
        out_shape=jax.ShapeDtypeStruct((b, d), F32),
        scratch_shapes=[pltpu.VMEM((bs, hd), F32)],
        compiler_params=_cparams("parallel"), name="mem_attn_sample",
    )(x, g[None], w_q.reshape(d, hd).astype(BF16), g_q[None], k, v, w_o.reshape(hd, d).astype(BF16))


def _ffn_body(x_ref, g_ref, w1_ref, w2_ref, o_ref, xn_sc, acc_sc):
    j = pl.program_id(1)

    @pl.when(j == 0)
    def _():
        xn_sc[...] = _rms(x_ref[...], g_ref[...]).astype(BF16)
        acc_sc[...] = jnp.zeros(acc_sc.shape, F32)

    h = jnp.maximum(jnp.dot(xn_sc[...], w1_ref[...], preferred_element_type=F32), 0.0)
    acc_sc[...] += _dot(h * h, w2_ref[...])

    @pl.when(j == pl.num_programs(1) - 1)
    def _():
        o_ref[...] = x_ref[...] + acc_sc[...]


def _ffn(x, g, w1, w2, tm, tf=512):
    m, d = x.shape
    ff = w1.shape[1]
    return pl.pallas_call(
        _ffn_body, grid=(m // tm, ff // tf),
        in_specs=[pl.BlockSpec((tm, d), lambda i, j: (i, 0)), _const_spec((1, d)),
                  pl.BlockSpec((d, tf), lambda i, j: (0, j)), pl.BlockSpec((tf, d), lambda i, j: (j, 0))],
        out_specs=pl.BlockSpec((tm, d), lambda i, j: (i, 0)),
        out_shape=jax.ShapeDtypeStruct((m, d), F32),
        scratch_shapes=[pltpu.VMEM((tm, d), BF16), pltpu.VMEM((tm, d), F32)],
        compiler_params=_cparams("parallel", "arbitrary"), name="ffn",
    )(x, g[None], w1, w2)


def _odd_weights(w_in, w_g, b_g, g_o, w_pool, pool_scale, w_out):
    nk = GLA_HEADS * GLA_DK
    nv = GLA_HEADS * GLA_DV
    pad = GLA_DVP - GLA_DV

    def pad_heads_cols(w):
        return jnp.pad(w.reshape(w.shape[0], GLA_HEADS, GLA_DV), ((0, 0), (0, 0), (0, pad))).reshape(
            w.shape[0], GLA_HEADS * GLA_DVP)

    o_v, o_r, o_g = 2 * nk, 2 * nk + nv, 2 * nk + 2 * nv
    o_u = o_g + GLA_GATE_RANK
    win = jnp.concatenate([
        w_in[:, :o_v], pad_heads_cols(w_in[:, o_v:o_r]), pad_heads_cols(w_in[:, o_r:o_g]),
        w_in[:, o_u:], jnp.pad(w_in[:, o_g:o_u], ((0, 0), (0, LANE - GLA_GATE_RANK)))], axis=1)
    wg = jnp.pad(w_g, ((0, LANE - GLA_GATE_RANK), (0, 0)))
    go = jnp.tile(jnp.pad(g_o, (0, pad)), GLA_HEADS)[None]
    wpool = jnp.zeros((POOL_DIM, POOL_DIM), F32)
    for gi in range(len(POOL_WINDOWS)):
        lo = gi * POOL_GDIM
        wpool = wpool.at[lo:lo + POOL_GDIM, lo:lo + POOL_GDIM].set(w_pool[gi])
    wo_gla = jnp.pad(w_out[:nv].reshape(GLA_HEADS, GLA_DV, -1), ((0, 0), (0, pad), (0, 0))).reshape(
        GLA_HEADS * GLA_DVP, -1)
    return dict(win=win.astype(BF16), wg=wg.astype(BF16), bg=b_g[None], go=go, wpool=wpool.astype(BF16),
                pscale=pool_scale[None], wo_gla=wo_gla.astype(BF16), wo_pool=w_out[nv:].astype(BF16))


def _odd_proj_body(x_ref, gmix_ref, win_ref, wg_ref, bg_ref, q_ref, k_ref, v_ref, r_ref, u_ref, g_ref):
    xn = _rms(x_ref[...], gmix_ref[...])
    z = _dot(xn, win_ref[...])
    nk = GLA_HEADS * GLA_DK
    nvp = GLA_HEADS * GLA_DVP
    q_ref[...] = z[:, :nk] * (GLA_DK ** -0.5)
    k_ref[...] = z[:, nk:2 * nk]
    v_ref[...] = z[:, 2 * nk:2 * nk + nvp]
    r_ref[...] = z[:, 2 * nk + nvp:2 * nk + 2 * nvp]
    o_u = 2 * nk + 2 * nvp
    u_ref[...] = z[:, o_u:o_u + POOL_DIM]
    zg = _dot(z[:, o_u + POOL_DIM:o_u + POOL_DIM + LANE], wg_ref[...]) + bg_ref[...]
    g_ref[...] = (jnp.minimum(zg, 0.0) - jnp.log1p(jnp.exp(-jnp.abs(zg)))) * (1.0 / GLA_TAU)


def _odd_proj(x, g_mix, ow, tm):
    m, d = x.shape
    nk = GLA_HEADS * GLA_DK
    nvp = GLA_HEADS * GLA_DVP
    row = lambda w: pl.BlockSpec((tm, w), lambda i: (i, 0))
    widths = [nk, nk, nvp, nvp, POOL_DIM, nk]
    return pl.pallas_call(
        _odd_proj_body, grid=(m // tm,),
        in_specs=[row(d), _const_spec((1, d)), _const_spec(ow["win"].shape), _const_spec(ow["wg"].shape),
                  _const_spec((1, nk))],
        out_specs=[row(w) for w in widths],
        out_shape=[jax.ShapeDtypeStruct((m, w), F32) for w in widths],
        compiler_params=_cparams("parallel"), name="odd_proj",
    )(x, g_mix[None], ow["win"], ow["wg"], ow["bg"])


def _gla_out(o, r, go):
    outs = []
    for h in range(GLA_HEADS):
        sl = slice(h * GLA_DVP, (h + 1) * GLA_DVP)
        oh = o[h] if isinstance(o, (list, tuple)) else o[:, sl]
        ms = jnp.sum(oh * oh, axis=-1, keepdims=True) * (1.0 / GLA_DV)
        rh = r[:, sl]
        outs.append(oh * lax.rsqrt(ms + EPS) * go[:, sl] * (rh / (1.0 + jnp.exp(-rh))))
    return jnp.concatenate(outs, axis=-1)


def _gla_scan_body(q_ref, k_ref, v_ref, g_ref, r_ref, go_ref, o_ref, st_ref, st_sc):
    c = pl.program_id(1)

    @pl.when(c == 0)
    def _():
        st_sc[...] = jnp.zeros(st_sc.shape, F32)

    cs = q_ref.shape[1]
    row = lax.broadcasted_iota(jnp.int32, (cs, cs), 0)
    col = lax.broadcasted_iota(jnp.int32, (cs, cs), 1)
    tril = (col <= row)
    tril_bf = tril.astype(BF16)
    outs = []
    for h in range(GLA_HEADS):
        ks = slice(h * GLA_DK, (h + 1) * GLA_DK)
        vs = slice(h * GLA_DVP, (h + 1) * GLA_DVP)
        q, k, v, g = q_ref[0, :, ks], k_ref[0, :, ks], v_ref[0, :, vs], g_ref[0, :, ks]
        g_hi = g.astype(BF16)
        g_r = g - g_hi.astype(F32)
        g_mid = g_r.astype(BF16)
        g_lo = (g_r - g_mid.astype(F32)).astype(BF16)
        gc = (jnp.dot(tril_bf, g_hi, preferred_element_type=F32)
              + jnp.dot(tril_bf, g_mid, preferred_element_type=F32)
              + jnp.dot(tril_bf, g_lo, preferred_element_type=F32))
        g_last = gc[cs - 1:cs, :]
        qd = q * jnp.exp(gc)
        a = jnp.where(tril, _dot_nt(qd, k * jnp.exp(-gc)), 0.0)
        st = st_sc[h]
        outs.append(_dot_nt(qd, st) + _dot(a, v))
        st_sc[h] = st * jnp.exp(g_last) + _dot_tn(v, k * jnp.exp(g_last - gc))
    o_ref[0] = _gla_out(outs, r_ref[0], go_ref[...])

    @pl.when(c == pl.num_programs(1) - 1)
    def _():
        st_ref[0] = st_sc[...]


def _gla_scan(q, k, v, g, r, go):
    n, s, nk = q.shape
    nvp = v.shape[2]
    cs = GLA_CHUNK
    blk = lambda w: pl.BlockSpec((1, cs, w), lambda b, c: (b, c, 0))
    return pl.pallas_call(
        _gla_scan_body, grid=(n, s // cs),
        in_specs=[blk(nk), blk(nk), blk(nvp), blk(nk), blk(nvp), _const_spec((1, nvp))],
        out_specs=[blk(nvp), pl.BlockSpec((1, GLA_HEADS, GLA_DVP, GLA_DK), lambda b, c: (b, 0, 0, 0))],
        out_shape=[jax.ShapeDtypeStruct((n, s, nvp), F32),
                   jax.ShapeDtypeStruct((n, GLA_HEADS, GLA_DVP, GLA_DK), F32)],
        scratch_shapes=[pltpu.VMEM((GLA_HEADS, GLA_DVP, GLA_DK), F32)],
        compiler_params=_cparams("parallel", "arbitrary"), name="gla_scan",
    )(q, k, v, g, r, go)


def _pool_group_select(vals):
    grp = lax.broadcasted_iota(jnp.int32, (1, POOL_DIM), 1) // POOL_GDIM
    out = vals[-1]
    for gi in range(len(vals) - 2, -1, -1):
        out = jnp.where(grp == gi, vals[gi], out)
    return out


def _pool_body(u_ref, wp_ref, ps_ref, o_ref, ext_sc, *, tm):
    i = pl.program_id(1)
    hist = POOL_HIST + 1

    @pl.when(i == 0)
    def _():
        ext_sc[0:hist, :] = jnp.zeros((hist, POOL_DIM), F32)

    u = u_ref[0]
    ext_sc[hist:hist + tm, :] = u
    acc = u
    sums = []
    for k in range(1, max(POOL_WINDOWS)):
        acc = acc + ext_sc[hist - k:hist - k + tm, :]
        if k + 1 in POOL_WINDOWS:
            sums.append(acc)
    pos = i * tm + lax.broadcasted_iota(jnp.int32, (tm, 1), 0)
    win = _pool_group_select([jnp.full((1, POOL_DIM), w, jnp.int32) for w in POOL_WINDOWS])
    cnt = jnp.minimum(pos + 1, win).astype(F32)
    diff = _pool_group_select(sums) / cnt - u
    o_ref[0] = _dot(diff, wp_ref[...]) * ps_ref[...]
    ext_sc[0:hist, :] = ext_sc[tm:tm + hist, :]


def _pool_prompt(u, wpool, pscale, tm=512):
    n, s, pd = u.shape
    blk = pl.BlockSpec((1, tm, pd), lambda b, i: (b, i, 0))
    return pl.pallas_call(
        functools.partial(_pool_body, tm=tm), grid=(n, s // tm),
        in_specs=[blk, _const_spec((pd, pd)), _const_spec((1, pd))],
        out_specs=blk, out_shape=jax.ShapeDtypeStruct((n, s, pd), F32),
        scratch_shapes=[pltpu.VMEM((tm + POOL_HIST + 1, pd), F32)],
        compiler_params=_cparams("parallel", "arbitrary"), name="pool_prompt",
    )(u, wpool, pscale)


def _pool_sample_body(ue_ref, wp_ref, ps_ref, o_ref, *, past):
    ue = ue_ref[...]
    t = ue.shape[1]
    rowi = lax.broadcasted_iota(jnp.int32, (1, t, 1), 1)
    sums = [jnp.sum(jnp.where(rowi >= t - w, ue, 0.0), axis=1) for w in POOL_WINDOWS]
    cnt = _pool_group_select(
        [jnp.full((1, POOL_DIM), float(min(past + 1, w)), F32) for w in POOL_WINDOWS])
    diff = _pool_group_select(sums) / cnt - ue[:, t - 1, :]
    o_ref[...] = _dot(diff, wp_ref[...]) * ps_ref[...]


def _pool_sample(u_ext, wpool, pscale, past):
    b, t, pd = u_ext.shape
    return pl.pallas_call(
        functools.partial(_pool_sample_body, past=past), grid=(1,),
        in_specs=[_const_spec((b, t, pd)), _const_spec((pd, pd)), _const_spec((1, pd))],
        out_specs=_const_spec((b, pd)), out_shape=jax.ShapeDtypeStruct((b, pd), F32),
        compiler_params=_cparams("arbitrary"), name="pool_sample",
    )(u_ext, wpool, pscale)


def _gla_sample_body(q_ref, k_ref, v_ref, g_ref, r_ref, go_ref, st_ref, o_ref, sn_ref, o_sc, *, bs):
    eye = (lax.broadcasted_iota(jnp.int32, (GLA_DK, GLA_DK), 0)
           == lax.broadcasted_iota(jnp.int32, (GLA_DK, GLA_DK), 1))

    def col(rowv):
        return jnp.sum(jnp.where(eye, rowv, 0.0), axis=1, keepdims=True)

    o_sc[...] = jnp.zeros(o_sc.shape, F32)
    for b in range(bs):
        for h in range(GLA_HEADS):
            ks = slice(h * GLA_DK, (h + 1) * GLA_DK)
            q = q_ref[b:b + 1, ks]
            k = k_ref[b:b + 1, ks]
            eg = jnp.exp(g_ref[b:b + 1, ks])
            v = v_ref[b:b + 1, h * GLA_DVP:h * GLA_DVP + GLA_DV]
            s_new = st_ref[b, h] * col(eg) + col(k) * v
            sn_ref[b, h] = s_new
            o_sc[b:b + 1, h * GLA_DVP:h * GLA_DVP + GLA_DV] = jnp.sum(col(q) * s_new, axis=0, keepdims=True)
    o_ref[...] = _gla_out(o_sc[...], r_ref[...], go_ref[...])


def _gla_sample(q, k, v, g, r, go, state, bs=8):
    b, nk = q.shape
    nvp = v.shape[1]
    row = lambda w: pl.BlockSpec((bs, w), lambda i: (i, 0))
    st = pl.BlockSpec((bs, GLA_HEADS, GLA_DK, GLA_DV), lambda i: (i, 0, 0, 0))
    return pl.pallas_call(
        functools.partial(_gla_sample_body, bs=bs), grid=(b // bs,),
        in_specs=[row(nk), row(nk), row(nvp), row(nk), row(nvp), _const_spec((1, nvp)), st],
        out_specs=[row(nvp), st],
        out_shape=[jax.ShapeDtypeStruct((b, nvp), F32), jax.ShapeDtypeStruct(state.shape, F32)],
        scratch_shapes=[pltpu.VMEM((bs, nvp), F32)],
        compiler_params=_cparams("parallel"), name="gla_sample",
    )(q, k, v, g, r, go, state)


def _sample_q_body(qa_ref, gak_ref, wk_ref, selr_ref, selsw_ref, qabs_ref, qr_ref, qsw_ref):
    qa = qa_ref[...]
    lane = lax.broadcasted_iota(jnp.int32, (1, qa.shape[1]), 1)
    qg = qa.astype(F32) * gak_ref[...]
    for h in range(MLA_HEADS):
        base = MLA_PAIR * (h // 2) + MLA_NOPE * (h % 2)
        nope = (lane >= base) & (lane < base + MLA_NOPE)
        qabs_ref[h] = _dot_nt(jnp.where(nope, qg, 0.0), wk_ref[...]).astype(BF16)
        qr_ref[h] = jnp.dot(qa, selr_ref[h], preferred_element_type=F32).astype(BF16)
        qsw_ref[h] = jnp.dot(qa, selsw_ref[h], preferred_element_type=F32).astype(BF16)


def _sample_q(qa, gak, wk):
    b, width = qa.shape
    cols = _mla_cols()
    selr = np.zeros((MLA_HEADS, width, MLA_ROPE), np.float32)
    selsw = np.zeros((MLA_HEADS, width, MLA_ROPE), np.float32)
    half = MLA_ROPE // 2
    for h in range(MLA_HEADS):
        rc = cols[h, MLA_NOPE:]
        selr[h, rc, np.arange(MLA_ROPE)] = 1.0
        selsw[h, rc[half:], np.arange(half)] = 1.0
        selsw[h, rc[:half], half + np.arange(half)] = -1.0
    outs = pl.pallas_call(
        _sample_q_body, grid=(1,),
        in_specs=[_const_spec(qa.shape), _const_spec(gak.shape), _const_spec(wk.shape),
                  _const_spec(selr.shape), _const_spec(selsw.shape)],
        out_specs=[_const_spec((MLA_HEADS, b, MLA_KV_LORA)), _const_spec((MLA_HEADS, b, MLA_ROPE)),
                   _const_spec((MLA_HEADS, b, MLA_ROPE))],
        out_shape=[jax.ShapeDtypeStruct((MLA_HEADS, b, MLA_KV_LORA), BF16),
                   jax.ShapeDtypeStruct((MLA_HEADS, b, MLA_ROPE), BF16),
                   jax.ShapeDtypeStruct((MLA_HEADS, b, MLA_ROPE), BF16)],
        compiler_params=_cparams("arbitrary"), name="sample_q",
    )(qa, gak, wk, jnp.asarray(selr, BF16), jnp.asarray(selsw, BF16))
    return [jnp.swapaxes(o, 0, 1) for o in outs]


def _paged_body(pt_ref, *refs, npg):
    del pt_ref
    pages = refs[:4 * npg]
    (qabs_ref, qr_ref, qsw_ref, qb_ref, ckvn_ref, krn_ref, tc_ref, ts_ref, tcn_ref, tsn_ref,
     wuk_ref, e64t_ref, ones_ref) = refs[4 * npg:4 * npg + 13]
    olat_ref, mb_ref, lb_ref, ob_ref, ks_ref = refs[4 * npg + 13:4 * npg + 18]
    m_sc, l_sc, acc_sc = refs[4 * npg + 18:]
    c = pl.program_id(1)
    rows = 8

    @pl.when(c == 0)
    def _():
        m_sc[...] = jnp.full(m_sc.shape, NEG_INF, F32)
        l_sc[...] = jnp.zeros(l_sc.shape, F32)
        acc_sc[...] = jnp.zeros(acc_sc.shape, F32)

    qabs, qr, qsw = qabs_ref[0], qr_ref[0], qsw_ref[0]

    def mla_rows(ckv, kr, tc, ts, n_valid):
        ckv_b = ckv.astype(BF16)
        kn = jnp.dot(ckv_b, wuk_ref[...], preferred_element_type=F32)
        ss = _dot_nt(e64t_ref[...], kn * kn) + _dot_nt(ones_ref[...], kr * kr)
        s = (_dot_nt(qabs, ckv_b) + _dot_nt(qr, kr * tc) + _dot_nt(qsw, kr * ts))
        s = s * lax.rsqrt(ss * (1.0 / MLA_QK) + EPS)
        if n_valid is not None:
            s = jnp.where(lax.broadcasted_iota(jnp.int32, s.shape, 1) < n_valid, s, NEG_INF)
        m_prev = m_sc[...]
        m_new = jnp.maximum(m_prev, jnp.max(s, axis=-1, keepdims=True))
        alpha = jnp.exp(m_prev - m_new)
        p = jnp.exp(s - m_new)
        l_sc[...] = alpha * l_sc[...] + jnp.sum(p, axis=-1, keepdims=True)
        acc_sc[...] = alpha * acc_sc[...] + jnp.dot(p.astype(BF16), ckv_b, preferred_element_type=F32)
        m_sc[...] = m_new

    hb = MOBA_HEADS * MOBA_HD
    hrow = lax.broadcasted_iota(jnp.int32, (rows, hb), 0)
    own = hrow == lax.broadcasted_iota(jnp.int32, (rows, hb), 1) // MOBA_HD
    qbd = jnp.where(own, jnp.broadcast_to(qb_ref[0], (rows, hb)), 0.0).astype(BF16)
    eye = (lax.broadcasted_iota(jnp.int32, (rows, LANE), 0)
           == lax.broadcasted_iota(jnp.int32, (rows, LANE), 1))

    def to_row(colv):
        return jnp.sum(jnp.where(eye, colv, 0.0), axis=0, keepdims=True)

    for j in range(npg):
        ckv_ref, kr_ref, kk_ref, vv_ref = pages[4 * j:4 * j + 4]
        sl = slice(j * PAGE_SIZE, (j + 1) * PAGE_SIZE)
        mla_rows(ckv_ref[0], kr_ref[0], tc_ref[sl, :], ts_ref[sl, :], None)
        kp = kk_ref[0]
        s = _dot_nt(qbd, kp) * (MOBA_HD ** -0.5)
        m = jnp.max(s, axis=-1, keepdims=True)
        p = jnp.exp(s - m)
        o = _dot(p, vv_ref[0])
        mb_ref[0, j:j + 1, :] = to_row(m)
        lb_ref[0, j:j + 1, :] = to_row(jnp.sum(p, axis=-1, keepdims=True))
        ob_ref[0, j:j + 1, :] = jnp.sum(jnp.where(own, o, 0.0), axis=0, keepdims=True)
        ks_ref[0, j:j + 1, :] = jnp.sum(kp, axis=0, keepdims=True)

    @pl.when(c == pl.num_programs(1) - 1)
    def _():
        mla_rows(ckvn_ref[0], krn_ref[0], tcn_ref[...], tsn_ref[...], 1)
        olat_ref[0] = acc_sc[...] / l_sc[...]


def _paged_attend(page_table, c_ckv, c_kr, c_k, c_v, qabs, qr, qsw, qb, ckv_new, kr_new, gk_rope,
                  wuk, npg=8):
    b, n_pages = page_table.shape
    past = n_pages * PAGE_SIZE
    hb = MOBA_HEADS * MOBA_HD
    rows = 8
    nstep = n_pages // npg
    c, s = _rope_tables(jnp.arange(past + 1), MLA_ROPE // 2, MLA_THETA)
    tc = jnp.concatenate([c, c], -1) * gk_rope[None]
    ts = jnp.concatenate([s, s], -1) * gk_rope[None]
    pad_new = lambda t: jnp.pad(t[:, None, :], ((0, 0), (0, rows - 1), (0, 0)))
    e64t = np.zeros((rows, MLA_HEADS * MLA_NOPE), np.float32)
    for h in range(MLA_HEADS):
        e64t[h, h * MLA_NOPE:(h + 1) * MLA_NOPE] = 1.0

    def page_spec(width, j):
        return pl.BlockSpec((1, PAGE_SIZE, width), lambda i, cc, pt, j=j: (pt[i, cc * npg + j], 0, 0))

    page_specs, page_args = [], []
    for j in range(npg):
        page_specs += [page_spec(MLA_KV_LORA, j), page_spec(MLA_ROPE, j), page_spec(hb, j), page_spec(hb, j)]
        page_args += [c_ckv, c_kr, c_k, c_v]
    per_sample = lambda r, w: pl.BlockSpec((1, r, w), lambda i, cc, pt: (i, 0, 0))
    const = lambda shape: pl.BlockSpec(shape, lambda i, cc, pt: (0,) * len(shape))
    tab = pl.BlockSpec((npg * PAGE_SIZE, MLA_ROPE), lambda i, cc, pt: (cc, 0))
    in_specs = page_specs + [
        per_sample(rows, MLA_KV_LORA), per_sample(rows, MLA_ROPE), per_sample(rows, MLA_ROPE),
        per_sample(1, hb), per_sample(rows, MLA_KV_LORA), per_sample(rows, MLA_ROPE),
        tab, tab, const((rows, MLA_ROPE)), const((rows, MLA_ROPE)),
        const(wuk.shape), const(e64t.shape), const((rows, MLA_ROPE))]
    pg = lambda w: pl.BlockSpec((1, npg, w), lambda i, cc, pt: (i, cc, 0))
    grid_spec = pltpu.PrefetchScalarGridSpec(
        num_scalar_prefetch=1, grid=(b, nstep), in_specs=in_specs,
        out_specs=[per_sample(rows, MLA_KV_LORA), pg(LANE), pg(LANE), pg(hb), pg(hb)],
        scratch_shapes=[pltpu.VMEM((rows, 1), F32), pltpu.VMEM((rows, 1), F32),
                        pltpu.VMEM((rows, MLA_KV_LORA), F32)])
    tnew = lambda t: jnp.broadcast_to(t[past:past + 1], (rows, MLA_ROPE))
    return pl.pallas_call(
        functools.partial(_paged_body, npg=npg), grid_spec=grid_spec,
        out_shape=[jax.ShapeDtypeStruct((b, rows, MLA_KV_LORA), F32),
                   jax.ShapeDtypeStruct((b, n_pages, LANE), F32),
                   jax.ShapeDtypeStruct((b, n_pages, LANE), F32),
                   jax.ShapeDtypeStruct((b, n_pages, hb), F32),
                   jax.ShapeDtypeStruct((b, n_pages, hb), F32)],
        compiler_params=_cparams("parallel", "arbitrary"), name="paged_attend",
    )(page_table, *page_args, qabs, qr, qsw, qb[:, None, :], pad_new(ckv_new), pad_new(kr_new),
      tc[:past], ts[:past], tnew(tc), tnew(ts), wuk, jnp.asarray(e64t, BF16),
      jnp.ones((rows, MLA_ROPE), BF16))


def _moba_combine_body(m_ref, l_ref, o_ref, ks_ref, q_ref, kn_ref, vn_ref, e_ref, et_ref, out_ref):
    hb = MOBA_HEADS * MOBA_HD
    q = q_ref[0]
    ks = ks_ref[0]
    kmean = (ks[:, :hb] + ks[:, hb:]) * (1.0 / MOBA_BLOCK)
    gate = _dot_hl(kmean * q, e_ref[...])
    sel = _top_blocks(gate, jnp.full(gate.shape, True), 0)
    m = m_ref[0]
    l = l_ref[0]
    o = o_ref[0]
    m0 = jnp.where(sel, m[:, :LANE], NEG_INF)
    m1 = jnp.where(sel, m[:, LANE:], NEG_INF)
    s_own = _dot_hl(q * kn_ref[0], e_ref[...]) * (MOBA_HD ** -0.5)
    mt = jnp.maximum(jnp.maximum(jnp.max(m0, axis=0, keepdims=True), jnp.max(m1, axis=0, keepdims=True)),
                     s_own)
    w0 = jnp.exp(m0 - mt)
    w1 = jnp.exp(m1 - mt)
    w_own = jnp.exp(s_own - mt)
    den = jnp.sum(w0 * l[:, :LANE] + w1 * l[:, LANE:], axis=0, keepdims=True) + w_own
    expand = lambda t: _dot_hl(t, et_ref[...])
    num = (jnp.sum(expand(w0) * o[:, :hb] + expand(w1) * o[:, hb:], axis=0, keepdims=True)
           + expand(w_own) * vn_ref[0])
    out_ref[0] = num / expand(den)


def _moba_combine(mb, lb, ob, ks, qb, kb_new, vb_new):
    b, n_pages, hb = ob.shape
    ppb = MOBA_BLOCK // PAGE_SIZE
    nb = n_pages // ppb
    e = np.zeros((hb, LANE), np.float32)
    for h in range(MOBA_HEADS):
        e[h * MOBA_HD:(h + 1) * MOBA_HD, h] = 1.0
    blk = lambda r, w: pl.BlockSpec((1, r, w), lambda i: (i, 0, 0))
    out = pl.pallas_call(
        _moba_combine_body, grid=(b,),
        in_specs=[blk(nb, ppb * LANE), blk(nb, ppb * LANE), blk(nb, ppb * hb), blk(nb, ppb * hb),
                  blk(1, hb), blk(1, hb), blk(1, hb), _const_spec((hb, LANE)), _const_spec((LANE, hb))],
        out_specs=blk(1, hb), out_shape=jax.ShapeDtypeStruct((b, 1, hb), F32),
        compiler_params=_cparams("parallel"), name="moba_combine",
    )(mb.reshape(b, nb, ppb * LANE), lb.reshape(b, nb, ppb * LANE), ob.reshape(b, nb, ppb * hb),
      ks.reshape(b, nb, ppb * hb), qb[:, None], kb_new[:, None], vb_new[:, None],
      jnp.asarray(e, BF16), jnp.asarray(e.T, BF16))
    return out[:, 0]


def _latent_out_body(olat_ref, wv_ref, o_ref):
    lane = lax.broadcasted_iota(jnp.int32, (1, MLA_HEADS * MLA_V), 1)
    wv = wv_ref[...]
    acc = jnp.zeros(o_ref.shape, F32)
    for h in range(MLA_HEADS):
        wh = jnp.where((lane >= h * MLA_V) & (lane < (h + 1) * MLA_V), wv, jnp.zeros_like(wv))
        acc = acc + _dot(olat_ref[h], wh)
    o_ref[...] = acc


def _latent_out(olat_hm, wv):
    _, b, _ = olat_hm.shape
    return pl.pallas_call(
        _latent_out_body, grid=(1,),
        in_specs=[_const_spec(olat_hm.shape), _const_spec(wv.shape)],
        out_specs=_const_spec((b, MLA_HEADS * MLA_V)),
        out_shape=jax.ShapeDtypeStruct((b, MLA_HEADS * MLA_V), F32),
        compiler_params=_cparams("arbitrary"), name="latent_out",
    )(olat_hm, wv)


def _even_layer(yp, ys, page_table, c_ckv, c_kr, c_k, c_v, g_mix, w_in, g_cq, g_ckv, w_uq, w_uk, w_uv,
                g_aq, g_ak, g_bq, g_bk, w_out):
    n, s, d = yp.shape
    b = ys.shape[0]
    past = page_table.shape[1] * PAGE_SIZE
    hb = MOBA_HEADS * MOBA_HD
    ew = _even_weights(w_in, g_cq, g_ckv, w_uq, w_uk, w_uv, g_aq, g_ak, g_bq, g_bk)
    wo_a = w_out[:MLA_HEADS * MLA_V].astype(BF16)
    wo_b = w_out[MLA_HEADS * MLA_V:].astype(BF16)

    ckv, kr, qa, ka, va, qb, kb, vb, km = _even_proj(yp.reshape(n * s, d), g_mix, ew, jnp.arange(s),
                                                     MOBA_BLOCK)
    r3 = lambda t: t.reshape(n, s, t.shape[-1])
    oa = _mla_attn(r3(qa), r3(ka), r3(va))
    ob = _moba_attn(r3(qb), r3(kb), r3(vb), km.reshape(n, s // MOBA_BLOCK, hb))
    yp_new = _mm2_res(oa.reshape(n * s, -1), ob.reshape(n * s, -1), wo_a, wo_b, yp.reshape(n * s, d),
                      512).reshape(n, s, d)
    p_caches = (r3(ckv), r3(kr), kb.reshape(n, s, MOBA_HEADS, MOBA_HD), vb.reshape(n, s, MOBA_HEADS, MOBA_HD))

    sckv, skr, sqa, _, _, sqb, skb, svb, _ = _even_proj(
        ys.reshape(b, d), g_mix, ew, jnp.full((b,), past, jnp.int32), b)
    qabs, qr, qsw = _sample_q(sqa, ew["gak"], ew["wk"])
    wuk = w_uk.reshape(MLA_KV_LORA, -1).astype(BF16)
    olat, mb, lb, obp, ks = _paged_attend(page_table, c_ckv, c_kr, c_k, c_v, qabs, qr, qsw, sqb, sckv, skr,
                                          g_ak[MLA_NOPE:], wuk)
    soa = _latent_out(jnp.swapaxes(olat, 0, 1), ew["wv"])
    sob = _moba_combine(mb, lb, obp, ks, sqb, skb, svb)
    ys_new = _mm2_res(soa, sob, wo_a, wo_b, ys.reshape(b, d), b).reshape(b, 1, d)
    s_caches = (sckv[:, None], skr[:, None], skb.reshape(b, 1, MOBA_HEADS, MOBA_HD),
                svb.reshape(b, 1, MOBA_HEADS, MOBA_HD))
    return yp_new, ys_new, p_caches, s_caches


def _odd_layer(yp, ys, past, s_prev, u_prev, g_mix, w_in, w_g, b_g, g_o, w_pool, pool_scale, w_out):
    n, s, d = yp.shape
    b = ys.shape[0]
    ow = _odd_weights(w_in, w_g, b_g, g_o, w_pool, pool_scale, w_out)

    q, k, v, r, u, g = _odd_proj(yp.reshape(n * s, d), g_mix, ow, 512)
    r3 = lambda t: t.reshape(n, s, t.shape[-1])
    og, st_t = _gla_scan(r3(q), r3(k), r3(v), r3(g), r3(r), ow["go"])
    u3 = r3(u)
    pm = _pool_prompt(u3, ow["wpool"], ow["pscale"])
    yp_new = _mm2_res(og.reshape(n * s, -1), pm.reshape(n * s, -1), ow["wo_gla"], ow["wo_pool"],
                      yp.reshape(n * s, d), 512).reshape(n, s, d)
    p_state = jnp.swapaxes(st_t, 2, 3)[..., :GLA_DV]
    p_pool = u3[:, s - POOL_HIST:]

    sq, sk, sv, sr, su, sg = _odd_proj(ys.reshape(b, d), g_mix, ow, b)
    sog, s_new = _gla_sample(sq, sk, sv, sg, sr, ow["go"], s_prev)
    u_ext = jnp.concatenate([u_prev, su[:, None]], axis=1)
    spm = _pool_sample(u_ext, ow["wpool"], ow["pscale"], past)
    ys_new = _mm2_res(sog, spm, ow["wo_gla"], ow["wo_pool"], ys.reshape(b, d), b).reshape(b, 1, d)
    return yp_new, ys_new, (p_state, p_pool), (s_new, u_ext[:, 1:])


def kernel(x_prompt, x_sample, cache_mla_ckv, cache_mla_krope, cache_moba_k, cache_moba_v, state_gla, state_pool, cache_mem_k, cache_mem_v, page_table, mem_prompt, g_mix, g_memx, g_memm, w_mq, w_mk, w_mv, g_mq, g_mk, w_mo, g_ffn, w_ff1, w_ff2, ev_w_in, ev_g_cq, ev_g_ckv, ev_w_uq, ev_w_uk, ev_w_uv, ev_g_mla_q, ev_g_mla_k, ev_g_moba_q, ev_g_moba_k, ev_w_out, od_w_in, od_w_gate, od_b_gate, od_g_gla, od_w_pool, od_pool_scale, od_w_out):
    depth = g_mix.shape[0]
    n, s, d = x_prompt.shape
    b = x_sample.shape[0]
    past = page_table.shape[1] * PAGE_SIZE
    hb = MOBA_HEADS * MOBA_HD
    hd = MEM_HEADS * MEM_HD
    yp, ys = x_prompt, x_sample
    outs = {name: [] for name in ("p_ckv", "p_kr", "p_k", "p_v", "s_ckv", "s_kr", "s_k", "s_v",
                                  "p_gla", "p_pool", "s_gla", "s_pool", "p_mk", "p_mv")}
    for l in range(depth):
        i = l // 2
        if l % 2 == 0:
            n_pool = cache_mla_ckv.shape[1]
            yp, ys, pc, sc = _even_layer(
                yp, ys, page_table, cache_mla_ckv[i], cache_mla_krope[i],
                cache_moba_k[i].reshape(n_pool, PAGE_SIZE, hb), cache_moba_v[i].reshape(n_pool, PAGE_SIZE, hb),
                g_mix[l], ev_w_in[i], ev_g_cq[i], ev_g_ckv[i], ev_w_uq[i], ev_w_uk[i], ev_w_uv[i],
                ev_g_mla_q[i], ev_g_mla_k[i], ev_g_moba_q[i], ev_g_moba_k[i], ev_w_out[i])
            for name, val in zip(("p_ckv", "p_kr", "p_k", "p_v"), pc):
                outs[name].append(val)
            for name, val in zip(("s_ckv", "s_kr", "s_k", "s_v"), sc):
                outs[name].append(val)
        else:
            yp, ys, ps, ss = _odd_layer(
                yp, ys, past, state_gla[i], state_pool[i], g_mix[l], od_w_in[i], od_w_gate[i], od_b_gate[i],
                od_g_gla[i], od_w_pool[i], od_pool_scale[i], od_w_out[i])
            for name, val in zip(("p_gla", "p_pool"), ps):
                outs[name].append(val)
            for name, val in zip(("s_gla", "s_pool"), ss):
                outs[name].append(val)
        mk, mv = _mem_kv(mem_prompt, g_memm[l], w_mk[l], w_mv[l], g_mk[l])
        ml = mk.shape[1]
        outs["p_mk"].append(mk.reshape(n, ml, MEM_HEADS, MEM_HD))
        outs["p_mv"].append(mv.reshape(n, ml, MEM_HEADS, MEM_HD))
        yp = _mem_attn(yp, g_memx[l], w_mq[l], g_mq[l], mk, mv, w_mo[l], 512)
        ys = _mem_attn_sample(ys.reshape(b, d), g_memx[l], w_mq[l], g_mq[l],
                              cache_mem_k[l].reshape(b, ml, hd), cache_mem_v[l].reshape(b, ml, hd),
                              w_mo[l]).reshape(b, 1, d)
        w1 = w_ff1[l].astype(BF16)
        w2 = w_ff2[l].astype(BF16)
        yp = _ffn(yp.reshape(n * s, d), g_ffn[l], w1, w2, 1024).reshape(n, s, d)
        ys = _ffn(ys.reshape(b, d), g_ffn[l], w1, w2, b).reshape(b, 1, d)
    st = lambda name: jnp.stack(outs[name])
    return (yp, ys, st("p_ckv"), st("p_kr"), st("p_k"), st("p_v"), st("p_gla"), st("p_pool"),
            st("p_mk"), st("p_mv"), st("s_ckv"), st("s_kr"), st("s_k"), st("s_v"), st("s_gla"), st("s_pool"))
```

```python
import functools
import math

import jax
import jax.numpy as jnp
import numpy as np
from jax import lax
from jax.experimental import pallas as pl
from jax.experimental.pallas import tpu as pltpu

F32 = jnp.float32
BF16 = jnp.bfloat16
NEG_INF = float("-inf")

EPS = 1e-6
D_MODEL = 1024
PAGE_SIZE = 128
MLA_HEADS = 8
MLA_NOPE = 64
MLA_ROPE = 32
MLA_QK = MLA_NOPE + MLA_ROPE
MLA_V = 64
MLA_Q_LORA = 384
MLA_KV_LORA = 256
MLA_THETA = 10000.0
MLA_PAIR = 256
MOBA_HEADS = 8
MOBA_HD = 64
MOBA_ROT = MOBA_HD // 4
MOBA_BLOCK = 256
MOBA_TOPK = 3
ROPE_THETA = 500000.0
GLA_HEADS = 4
GLA_DK = 128
GLA_DV = 192
GLA_DVP = 256
GLA_GATE_RANK = 16
GLA_TAU = 16.0
GLA_CHUNK = 64
POOL_WINDOWS = (2, 4, 8, 16)
POOL_GDIM = 64
POOL_DIM = POOL_GDIM * len(POOL_WINDOWS)
POOL_HIST = max(POOL_WINDOWS) - 1
MEM_HEADS = 4
MEM_HD = 128
LANE = 128
VMEM_LIMIT = 56 * 1024 * 1024


def _rms(x, g):
    return x * lax.rsqrt(jnp.mean(x * x, axis=-1, keepdims=True) + EPS) * g


def _dot(a, b):
    return jnp.dot(a.astype(BF16), b.astype(BF16), preferred_element_type=F32)


def _dot_nt(a, b):
    return lax.dot_general(a.astype(BF16), b.astype(BF16), (((1,), (1,)), ((), ())),
                           preferred_element_type=F32)


def _dot_tn(a, b):
    return lax.dot_general(a.astype(BF16), b.astype(BF16), (((0,), (0,)), ((), ())),
                           preferred_element_type=F32)


def _split(a):
    hi = a.astype(BF16)
    lo = (a - hi.astype(F32)).astype(BF16)
    return hi, lo


def _dot_hl(a, sel):
    hi, lo = _split(a)
    return (jnp.dot(hi, sel, preferred_element_type=F32)
            + jnp.dot(lo, sel, preferred_element_type=F32))


def _cparams(*sem):
    return pltpu.CompilerParams(dimension_semantics=sem, vmem_limit_bytes=VMEM_LIMIT)


def _const_spec(shape):
    nd = len(shape)
    return pl.BlockSpec(shape, lambda *_: (0,) * nd)


def _mla_cols():
    cols = np.zeros((MLA_HEADS, MLA_QK), np.int32)
    for h in range(MLA_HEADS):
        base = MLA_PAIR * (h // 2)
        cols[h, :MLA_NOPE] = base + MLA_NOPE * (h % 2) + np.arange(MLA_NOPE)
        cols[h, MLA_NOPE:] = base + 2 * MLA_NOPE + MLA_ROPE * (h % 2) + np.arange(MLA_ROPE)
    return cols


def _rope_tables(pos, half, theta):
    inv = jnp.exp(jnp.arange(half, dtype=F32) * (-math.log(theta) / half))
    ang = pos.astype(F32)[:, None] * inv[None, :]
    return jnp.cos(ang), jnp.sin(ang)


def _mla_pair_tables(pos):
    c, s = _rope_tables(pos, MLA_ROPE // 2, MLA_THETA)
    p = pos.shape[0]
    one = jnp.ones((p, 2 * MLA_NOPE), F32)
    pad1 = jnp.ones((p, MLA_PAIR - 2 * MLA_NOPE - 2 * MLA_ROPE), F32)
    ct = jnp.concatenate([one, c, c, c, c, pad1], -1)
    st = jnp.concatenate([0 * one, s, s, s, s, 0 * pad1], -1)
    return ct, st


def _moba_pair_tables(pos):
    c, s = _rope_tables(pos, MOBA_ROT // 2, ROPE_THETA)
    p = pos.shape[0]
    one = jnp.ones((p, MOBA_HD - MOBA_ROT), F32)
    ch = jnp.concatenate([c, c, one], -1)
    sh = jnp.concatenate([s, s, 0 * one], -1)
    return jnp.concatenate([ch, ch], -1), jnp.concatenate([sh, sh], -1)


def _even_weights(w_in, g_cq, g_ckv, w_uq, w_uk, w_uv, g_aq, g_ak, g_bq, g_bk):
    i0 = MLA_Q_LORA + MLA_KV_LORA
    i1 = i0 + MLA_ROPE
    idx = np.concatenate([np.arange(0, i0), np.arange(i1, w_in.shape[1]), np.arange(i0, i1)])
    win = jnp.pad(w_in[:, idx], ((0, 0), (0, LANE - MLA_ROPE))).astype(BF16)
    cols = _mla_cols()
    width = MLA_PAIR * MLA_HEADS // 2
    wq = jnp.zeros((MLA_Q_LORA, width), F32).at[:, cols.reshape(-1)].set(
        w_uq.reshape(MLA_Q_LORA, -1)).astype(BF16)
    wk = jnp.zeros((MLA_KV_LORA, width), F32).at[:, cols[:, :MLA_NOPE].reshape(-1)].set(
        w_uk.reshape(MLA_KV_LORA, -1)).astype(BF16)
    wv = w_uv.reshape(MLA_KV_LORA, -1).astype(BF16)
    pk = np.zeros((MLA_ROPE, width), np.float32)
    e = np.zeros((width, LANE), np.float32)
    for h in range(MLA_HEADS):
        pk[np.arange(MLA_ROPE), cols[h, MLA_NOPE:]] = 1.0
        e[cols[h], h] = 1.0
    gaq = jnp.zeros((1, width), F32).at[0, cols.reshape(-1)].set(jnp.tile(g_aq, MLA_HEADS))
    gak = jnp.zeros((1, width), F32).at[0, cols.reshape(-1)].set(jnp.tile(g_ak, MLA_HEADS))
    b64 = np.kron(np.eye(MOBA_HEADS, dtype=np.float32), np.ones((MOBA_HD, MOBA_HD), np.float32))
    return dict(
        win=win, gcq=g_cq[None], gckv=g_ckv[None], wq=wq, wk=wk, wv=wv,
        pk=jnp.asarray(pk, BF16), e=jnp.asarray(e, BF16), et=jnp.asarray(e.T, BF16),
        gaq=gaq, gak=gak, b64=jnp.asarray(b64, BF16),
        gbq=jnp.tile(g_bq, MOBA_HEADS)[None], gbk=jnp.tile(g_bk, MOBA_HEADS)[None])


def _even_proj_body(x_ref, gmix_ref, win_ref, gcq_ref, gckv_ref, wq_ref, wk_ref, wv_ref, pk_ref,
                    e_ref, et_ref, gaq_ref, gak_ref, ca_ref, sa_ref, b64_ref, gbq_ref, gbk_ref,
                    cb_ref, sb_ref,
                    ckv_ref, kr_ref, qa_ref, ka_ref, va_ref, qb_ref, kb_ref, vb_ref, km_ref):
    xn = _rms(x_ref[...], gmix_ref[...])
    z = _dot(xn, win_ref[...])
    o_ckv = MLA_Q_LORA
    o_qb = o_ckv + MLA_KV_LORA
    hb = MOBA_HEADS * MOBA_HD
    cqn = _rms(z[:, :o_ckv], gcq_ref[...])
    ckv = _rms(z[:, o_ckv:o_qb], gckv_ref[...])
    qb = z[:, o_qb:o_qb + hb]
    kb = z[:, o_qb + hb:o_qb + 2 * hb]
    vb = z[:, o_qb + 2 * hb:o_qb + 3 * hb]
    kr = z[:, o_qb + 3 * hb:o_qb + 3 * hb + MLA_ROPE]
    ckv_ref[...] = ckv
    kr_ref[...] = kr
    vb_ref[...] = vb

    npair = MLA_HEADS // 2
    ca = jnp.concatenate([ca_ref[...]] * npair, axis=-1)
    sa = jnp.concatenate([sa_ref[...]] * npair, axis=-1)
    width = MLA_PAIR * npair
    lane = lax.broadcasted_iota(jnp.int32, (1, width), 1)
    in_pair = lane % MLA_PAIR
    first_half = ((in_pair >= 2 * MLA_NOPE) & (in_pair < 2 * MLA_NOPE + 2 * MLA_ROPE)
                  & (lane % MLA_ROPE < MLA_ROPE // 2))

    def norm_rope_a(t, g):
        ss = _dot_hl(t * t, e_ref[...])
        rs = lax.rsqrt(ss * (1.0 / MLA_QK) + EPS)
        tn = t * _dot_hl(rs, et_ref[...]) * g
        sw = jnp.where(first_half, -pltpu.roll(tn, width - MLA_ROPE // 2, 1),
                       pltpu.roll(tn, MLA_ROPE // 2, 1))
        return tn * ca + sw * sa

    qa = norm_rope_a(_dot(cqn, wq_ref[...]), gaq_ref[...]) * (MLA_QK ** -0.5)
    ka = norm_rope_a(_dot(ckv, wk_ref[...]) + _dot_hl(kr, pk_ref[...]), gak_ref[...])
    qa_ref[...] = qa.astype(BF16)
    ka_ref[...] = ka.astype(BF16)
    va_ref[...] = _dot(ckv, wv_ref[...]).astype(BF16)

    cb = jnp.concatenate([cb_ref[...]] * (MOBA_HEADS // 2), axis=-1)
    sb = jnp.concatenate([sb_ref[...]] * (MOBA_HEADS // 2), axis=-1)
    lane_b = lax.broadcasted_iota(jnp.int32, (1, hb), 1)
    first_half_b = lane_b % MOBA_HD < MOBA_ROT // 2

    def norm_rope_b(t, g):
        ss = _dot_hl(t * t, b64_ref[...])
        tn = t * lax.rsqrt(ss * (1.0 / MOBA_HD) + EPS) * g
        sw = jnp.where(first_half_b, -pltpu.roll(tn, hb - MOBA_ROT // 2, 1),
                       pltpu.roll(tn, MOBA_ROT // 2, 1))
        return tn * cb + sw * sb

    qb_ref[...] = norm_rope_b(qb, gbq_ref[...])
    kbn = norm_rope_b(kb, gbk_ref[...])
    kb_ref[...] = kbn
    km_ref[0] = jnp.mean(kbn, axis=0, keepdims=True)


def _even_proj(x, g_mix, ew, pos, tm):
    m, d = x.shape
    nt = m // tm
    ca, sa = _mla_pair_tables(pos)
    cb, sb = _moba_pair_tables(pos)
    npt = pos.shape[0] // tm
    row = lambda w: pl.BlockSpec((tm, w), lambda i: (i, 0))
    tab = lambda w: pl.BlockSpec((tm, w), lambda i: (i % npt, 0))
    hb = MOBA_HEADS * MOBA_HD
    wa = MLA_PAIR * MLA_HEADS // 2
    consts = [g_mix[None], ew["win"], ew["gcq"], ew["gckv"], ew["wq"], ew["wk"], ew["wv"], ew["pk"],
              ew["e"], ew["et"], ew["gaq"], ew["gak"]]
    consts2 = [ew["b64"], ew["gbq"], ew["gbk"]]
    in_specs = ([row(d)] + [_const_spec(c.shape) for c in consts] + [tab(MLA_PAIR), tab(MLA_PAIR)]
                + [_const_spec(c.shape) for c in consts2] + [tab(LANE), tab(LANE)])
    out_shape = [
        jax.ShapeDtypeStruct((m, MLA_KV_LORA), F32), jax.ShapeDtypeStruct((m, MLA_ROPE), F32),
        jax.ShapeDtypeStruct((m, wa), BF16), jax.ShapeDtypeStruct((m, wa), BF16),
        jax.ShapeDtypeStruct((m, MLA_HEADS * MLA_V), BF16),
        jax.ShapeDtypeStruct((m, hb), F32), jax.ShapeDtypeStruct((m, hb), F32),
        jax.ShapeDtypeStruct((m, hb), F32), jax.ShapeDtypeStruct((nt, 1, hb), F32)]
    out_specs = [row(MLA_KV_LORA), row(MLA_ROPE), row(wa), row(wa), row(MLA_HEADS * MLA_V),
                 row(hb), row(hb), row(hb), pl.BlockSpec((1, 1, hb), lambda i: (i, 0, 0))]
    return pl.pallas_call(
        _even_proj_body, grid=(nt,), in_specs=in_specs, out_specs=out_specs, out_shape=out_shape,
        compiler_params=_cparams("parallel"), name="even_proj",
    )(x, *consts, ca, sa, *consts2, cb, sb)


def _mla_head_mask(e):
    lane = lax.broadcasted_iota(jnp.int32, (1, MLA_PAIR), 1)
    nope = (lane >= MLA_NOPE * e) & (lane < MLA_NOPE * (e + 1))
    r0 = 2 * MLA_NOPE + MLA_ROPE * e
    return nope | ((lane >= r0) & (lane < r0 + MLA_ROPE))


def _mla_attn_body(q_ref, k_ref, v_ref, o_ref, s_sc, mrun_sc, lrun_sc, acc_sc, *, tq):
    qi = pl.program_id(2)
    q = q_ref[0]
    qs = [jnp.where(_mla_head_mask(e), q, jnp.zeros_like(q)) for e in (0, 1)]
    causal = (lax.broadcasted_iota(jnp.int32, (tq, tq), 1)
              <= lax.broadcasted_iota(jnp.int32, (tq, tq), 0))
    mrun_sc[...] = jnp.full(mrun_sc.shape, NEG_INF, F32)

    def scores(ki, diagonal):
        k = k_ref[0, pl.ds(pl.multiple_of(ki * tq, tq), tq), :]
        for e in (0, 1):
            s = _dot_nt(qs[e], k)
            if diagonal:
                s = jnp.where(causal, s, NEG_INF)
            s_sc[e, ki] = s
            mrun_sc[e] = jnp.maximum(mrun_sc[e], s)

    def score_step(ki, carry):
        scores(ki, False)
        return carry

    lax.fori_loop(0, qi, score_step, 0)
    scores(qi, True)
    for e in (0, 1):
        mrun_sc[e] = jnp.broadcast_to(jnp.max(mrun_sc[e], axis=-1, keepdims=True), (tq, tq))
    lrun_sc[...] = jnp.zeros(lrun_sc.shape, F32)
    acc_sc[...] = jnp.zeros(acc_sc.shape, F32)

    def value_step(ki, carry):
        v = v_ref[0, pl.ds(pl.multiple_of(ki * tq, tq), tq), :]
        for e in (0, 1):
            p = jnp.exp(s_sc[e, ki] - mrun_sc[e])
            lrun_sc[e] += p
            acc_sc[e] += _dot(p, v)
        return carry

    lax.fori_loop(0, qi + 1, value_step, 0)
    lane_o = lax.broadcasted_iota(jnp.int32, (1, 2 * MLA_V), 1)
    o0 = acc_sc[0] / jnp.sum(lrun_sc[0], axis=-1, keepdims=True)
    o1 = acc_sc[1] / jnp.sum(lrun_sc[1], axis=-1, keepdims=True)
    o_ref[0] = jnp.where(lane_o < MLA_V, o0, o1)


def _mla_attn(qa, ka, va, tq=512):
    n, s, _ = qa.shape
    npair = MLA_HEADS // 2
    return pl.pallas_call(
        functools.partial(_mla_attn_body, tq=tq),
        grid=(n, npair, s // tq),
        in_specs=[pl.BlockSpec((1, tq, MLA_PAIR), lambda b, p, i: (b, i, p)),
                  pl.BlockSpec((1, s, MLA_PAIR), lambda b, p, i: (b, 0, p)),
                  pl.BlockSpec((1, s, 2 * MLA_V), lambda b, p, i: (b, 0, p))],
        out_specs=pl.BlockSpec((1, tq, 2 * MLA_V), lambda b, p, i: (b, i, p)),
        out_shape=jax.ShapeDtypeStruct((n, s, MLA_HEADS * MLA_V), F32),
        scratch_shapes=[pltpu.VMEM((2, s // tq, tq, tq), F32), pltpu.VMEM((2, tq, tq), F32),
                        pltpu.VMEM((2, tq, tq), F32), pltpu.VMEM((2, tq, 2 * MLA_V), F32)],
        compiler_params=_cparams("parallel", "parallel", "arbitrary"), name="mla_attn",
    )(qa, ka, va)


def _top_blocks(gate, eligible, axis):
    idx = lax.broadcasted_iota(jnp.int32, gate.shape, axis)
    big = jnp.int32(1 << 20)
    g = jnp.where(eligible, gate, NEG_INF)
    sel = jnp.zeros(gate.shape, jnp.bool_)
    for _ in range(MOBA_TOPK):
        mx = jnp.max(g, axis=axis, keepdims=True)
        cand = (g == mx) & (mx > NEG_INF)
        first = jnp.min(jnp.where(cand, idx, big), axis=axis, keepdims=True)
        pick = idx == first
        sel = sel | pick
        g = jnp.where(pick, NEG_INF, g)
    return sel


def _moba_attn_body(q_ref, k_ref, v_ref, km_ref, o_ref, s_sc, selb_sc, mrun_sc, lrun_sc, acc_sc, *, tq):
    qi = pl.program_id(2)
    q = q_ref[0]
    lane = lax.broadcasted_iota(jnp.int32, (1, 2 * MOBA_HD), 1)
    nblk = km_ref.shape[1]
    blk = lax.broadcasted_iota(jnp.int32, (tq, nblk), 1)
    km_hi, km_lo = _split(km_ref[0])
    qs = []
    for e in (0, 1):
        qe = jnp.where((lane >= MOBA_HD * e) & (lane < MOBA_HD * (e + 1)), q, 0.0)
        qs.append((qe * (MOBA_HD ** -0.5)).astype(BF16))
        q_hi, q_lo = _split(qe)
        nt = (((1,), (1,)), ((), ()))
        gate = (lax.dot_general(q_hi, km_hi, nt, preferred_element_type=F32)
                + lax.dot_general(q_hi, km_lo, nt, preferred_element_type=F32)
                + lax.dot_general(q_lo, km_hi, nt, preferred_element_type=F32))
        sel = _top_blocks(gate, blk < qi, 1).astype(F32)
        for j in range(nblk):
            selb_sc[e, j] = jnp.broadcast_to(sel[:, j:j + 1], (tq, LANE))
    causal = (lax.broadcasted_iota(jnp.int32, (tq, tq), 1)
              <= lax.broadcasted_iota(jnp.int32, (tq, tq), 0))

    def scores(j, own):
        k = k_ref[0, pl.ds(pl.multiple_of(j * tq, tq), tq), :].astype(BF16)
        for e in (0, 1):
            ok = causal if own else jnp.concatenate([selb_sc[e, j]] * (tq // LANE), axis=1) > 0.5
            s = jnp.where(ok, _dot_nt(qs[e], k), NEG_INF)
            s_sc[e, j] = s
            if own:
                mrun_sc[e] = s
            else:
                mrun_sc[e] = jnp.maximum(mrun_sc[e], s)

    def score_step(j, carry):
        scores(j, False)
        return carry

    scores(qi, True)
    lax.fori_loop(0, qi, score_step, 0)
    for e in (0, 1):
        mrun_sc[e] = jnp.broadcast_to(jnp.max(mrun_sc[e], axis=-1, keepdims=True), (tq, tq))
    lrun_sc[...] = jnp.zeros(lrun_sc.shape, F32)
    acc_sc[...] = jnp.zeros(acc_sc.shape, F32)

    def value_step(j, carry):
        v = v_ref[0, pl.ds(pl.multiple_of(j * tq, tq), tq), :].astype(BF16)
        for e in (0, 1):
            p = jnp.exp(s_sc[e, j] - mrun_sc[e])
            lrun_sc[e] += p
            acc_sc[e] += _dot(p, v)
        return carry

    lax.fori_loop(0, qi + 1, value_step, 0)
    o0 = acc_sc[0] / jnp.sum(lrun_sc[0], axis=-1, keepdims=True)
    o1 = acc_sc[1] / jnp.sum(lrun_sc[1], axis=-1, keepdims=True)
    o_ref[0] = jnp.where(lane < MOBA_HD, o0, o1)


def _moba_attn(qb, kb, vb, km):
    n, s, hb = qb.shape
    tq = MOBA_BLOCK
    nblk = s // tq
    return pl.pallas_call(
        functools.partial(_moba_attn_body, tq=tq),
        grid=(n, hb // LANE, nblk),
        in_specs=[pl.BlockSpec((1, tq, LANE), lambda b, p, i: (b, i, p)),
                  pl.BlockSpec((1, s, LANE), lambda b, p, i: (b, 0, p)),
                  pl.BlockSpec((1, s, LANE), lambda b, p, i: (b, 0, p)),
                  pl.BlockSpec((1, nblk, LANE), lambda b, p, i: (b, 0, p))],
        out_specs=pl.BlockSpec((1, tq, LANE), lambda b, p, i: (b, i, p)),
        out_shape=jax.ShapeDtypeStruct((n, s, hb), F32),
        scratch_shapes=[pltpu.VMEM((2, nblk, tq, tq), F32), pltpu.VMEM((2, nblk, tq, LANE), F32),
                        pltpu.VMEM((2, tq, tq), F32), pltpu.VMEM((2, tq, tq), F32),
                        pltpu.VMEM((2, tq, LANE), F32)],
        compiler_params=_cparams("parallel", "parallel", "arbitrary"), name="moba_attn",
    )(qb, kb, vb, km)


def _mm2_res_body(a_ref, b_ref, wa_ref, wb_ref, r_ref, o_ref):
    o_ref[...] = r_ref[...] + _dot(a_ref[...], wa_ref[...]) + _dot(b_ref[...], wb_ref[...])


def _mm2_res(a, b, wa, wb, res, tm):
    m, d = res.shape
    row = lambda w: pl.BlockSpec((tm, w), lambda i: (i, 0))
    return pl.pallas_call(
        _mm2_res_body, grid=(m // tm,),
        in_specs=[row(a.shape[1]), row(b.shape[1]), _const_spec(wa.shape), _const_spec(wb.shape), row(d)],
        out_specs=row(d), out_shape=jax.ShapeDtypeStruct((m, d), F32),
        compiler_params=_cparams("parallel"), name="mix_out",
    )(a, b, wa, wb, res)


def _mem_kv_body(mem_ref, gm_ref, wk_ref, wv_ref, gk_ref, k_ref, v_ref):
    mn = _rms(mem_ref[0], gm_ref[...])
    k = _dot(mn, wk_ref[...])
    for h in range(MEM_HEADS):
        sl = slice(h * MEM_HD, (h + 1) * MEM_HD)
        k_ref[0, :, sl] = _rms(k[:, sl], gk_ref[...])
    v_ref[0] = _dot(mn, wv_ref[...])


def _mem_kv(mem, g_m, w_k, w_v, g_k):
    n, ml, d = mem.shape
    hd = MEM_HEADS * MEM_HD
    blk = lambda w: pl.BlockSpec((1, ml, w), lambda b: (b, 0, 0))
    return pl.pallas_call(
        _mem_kv_body, grid=(n,),
        in_specs=[blk(d), _const_spec((1, d)), _const_spec((d, hd)), _const_spec((d, hd)),
                  _const_spec((1, MEM_HD))],
        out_specs=[blk(hd), blk(hd)],
        out_shape=[jax.ShapeDtypeStruct((n, ml, hd), F32)] * 2,
        compiler_params=_cparams("parallel"), name="mem_kv",
    )(mem, g_m[None], w_k.reshape(d, hd).astype(BF16), w_v.reshape(d, hd).astype(BF16), g_k[None])


def _mem_attn_body(x_ref, g_ref, wq_ref, gq_ref, k_ref, v_ref, wo_ref, o_ref):
    x = x_ref[0]
    q = _dot(_rms(x, g_ref[...]), wq_ref[...])
    k = k_ref[0].astype(BF16)
    v = v_ref[0].astype(BF16)
    outs = []
    for h in range(MEM_HEADS):
        sl = slice(h * MEM_HD, (h + 1) * MEM_HD)
        qh = _rms(q[:, sl], gq_ref[...])
        s = _dot_nt(qh, k[:, sl]) * (MEM_HD ** -0.5)
        p = jnp.exp(s - jnp.max(s, axis=-1, keepdims=True))
        p = p / jnp.sum(p, axis=-1, keepdims=True)
        outs.append(_dot(p, v[:, sl]))
    o_ref[0] = x + _dot(jnp.concatenate(outs, axis=-1), wo_ref[...])


def _mem_attn(x, g, w_q, g_q, k, v, w_o, tm):
    n, s, d = x.shape
    ml, hd = k.shape[1:]
    return pl.pallas_call(
        _mem_attn_body, grid=(n, s // tm),
        in_specs=[pl.BlockSpec((1, tm, d), lambda b, i: (b, i, 0)), _const_spec((1, d)),
                  _const_spec((d, hd)), _const_spec((1, MEM_HD)),
                  pl.BlockSpec((1, ml, hd), lambda b, i: (b, 0, 0)),
                  pl.BlockSpec((1, ml, hd), lambda b, i: (b, 0, 0)), _const_spec((hd, d))],
        out_specs=pl.BlockSpec((1, tm, d), lambda b, i: (b, i, 0)),
        out_shape=jax.ShapeDtypeStruct((n, s, d), F32),
        compiler_params=_cparams("parallel", "parallel"), name="mem_attn",
    )(x, g[None], w_q.reshape(d, hd).astype(BF16), g_q[None], k, v, w_o.reshape(hd, d).astype(BF16))


def _mem_attn_sample_body(x_ref, g_ref, wq_ref, gq_ref, k_ref, v_ref, wo_ref, o_ref, q_sc, att_sc, *, bs):
    x = x_ref[...]
    q = _dot(_rms(x, g_ref[...]), wq_ref[...])
    for h in range(MEM_HEADS):
        q_sc[:, h, :] = _rms(q[:, h * MEM_HD:(h + 1) * MEM_HD], gq_ref[...]) * (MEM_HD ** -0.5)
    for b in range(bs):
        s = jnp.sum(k_ref[b] * q_sc[b], axis=-1, keepdims=True)
        p = jnp.exp(s - jnp.max(s, axis=0, keepdims=True))
        o = jnp.sum(p * v_ref[b], axis=0) / jnp.sum(p, axis=0)
        for h in range(MEM_HEADS):
            att_sc[b:b + 1, h * MEM_HD:(h + 1) * MEM_HD] = o[h:h + 1, :]
    o_ref[...] = x + _dot(att_sc[...], wo_ref[...])


def _mem_attn_sample(x, g, w_q, g_q, cache_k, cache_v, layer, w_o, bs=8):
    b, d = x.shape
    ml = cache_k.shape[2]
    hd = MEM_HEADS * MEM_HD
    mem = pl.BlockSpec((None, bs, ml, MEM_HEADS, MEM_HD), lambda i: (layer, i, 0, 0, 0))
    return pl.pallas_call(
        functools.partial(_mem_attn_sample_body, bs=bs), grid=(b // bs,),
        in_specs=[pl.BlockSpec((bs, d), lambda i: (i, 0)), _const_spec((1, d)),
                  _const_spec((d, hd)), _const_spec((1, MEM_HD)), mem, mem, _const_spec((hd, d))],
        out_specs=pl.BlockSpec((bs, d), lambda i: (i, 0)),
        out_shape=jax.ShapeDtypeStruct((b, d), F32),
        scratch_shapes=[pltpu.VMEM((bs, MEM_HEADS, MEM_HD), F32), pltpu.VMEM((bs, hd), F32)],
        compiler_params=_cparams("parallel"), name="mem_attn_sample",
    )(x, g[None], w_q.reshape(d, hd).astype(BF16), g_q[None], cache_k, cache_v,
      w_o.reshape(hd, d).astype(BF16))


def _ffn_body(x_ref, g_ref, w1_ref, w2_ref, o_ref, xn_sc, acc_sc):
    j = pl.program_id(1)

    @pl.when(j == 0)
    def _():
        xn_sc[...] = _rms(x_ref[...], g_ref[...]).astype(BF16)
        acc_sc[...] = jnp.zeros(acc_sc.shape, F32)

    h = jnp.maximum(jnp.dot(xn_sc[...], w1_ref[...], preferred_element_type=F32), 0.0)
    acc_sc[...] += _dot(h * h, w2_ref[...])

    @pl.when(j == pl.num_programs(1) - 1)
    def _():
        o_ref[...] = x_ref[...] + acc_sc[...]


def _ffn(x, g, w1, w2, tm, tf=512):
    m, d = x.shape
    ff = w1.shape[1]
    return pl.pallas_call(
        _ffn_body, grid=(m // tm, ff // tf),
        in_specs=[pl.BlockSpec((tm, d), lambda i, j: (i, 0)), _const_spec((1, d)),
                  pl.BlockSpec((d, tf), lambda i, j: (0, j)), pl.BlockSpec((tf, d), lambda i, j: (j, 0))],
        out_specs=pl.BlockSpec((tm, d), lambda i, j: (i, 0)),
        out_shape=jax.ShapeDtypeStruct((m, d), F32),
        scratch_shapes=[pltpu.VMEM((tm, d), BF16), pltpu.VMEM((tm, d), F32)],
        compiler_params=_cparams("parallel", "arbitrary"), name="ffn",
    )(x, g[None], w1, w2)


def _odd_weights(w_in, w_g, b_g, g_o, w_pool, pool_scale, w_out):
    nk = GLA_HEADS * GLA_DK
    nv = GLA_HEADS * GLA_DV
    pad = GLA_DVP - GLA_DV

    def pad_heads_cols(w):
        return jnp.pad(w.reshape(w.shape[0], GLA_HEADS, GLA_DV), ((0, 0), (0, 0), (0, pad))).reshape(
            w.shape[0], GLA_HEADS * GLA_DVP)

    o_v, o_r, o_g = 2 * nk, 2 * nk + nv, 2 * nk + 2 * nv
    o_u = o_g + GLA_GATE_RANK
    win = jnp.concatenate([
        w_in[:, :o_v], pad_heads_cols(w_in[:, o_v:o_r]), pad_heads_cols(w_in[:, o_r:o_g]),
        w_in[:, o_u:], jnp.pad(w_in[:, o_g:o_u], ((0, 0), (0, LANE - GLA_GATE_RANK)))], axis=1)
    wg = jnp.pad(w_g, ((0, LANE - GLA_GATE_RANK), (0, 0)))
    go = jnp.tile(jnp.pad(g_o, (0, pad)), GLA_HEADS)[None]
    wpool = jnp.zeros((POOL_DIM, POOL_DIM), F32)
    for gi in range(len(POOL_WINDOWS)):
        lo = gi * POOL_GDIM
        wpool = wpool.at[lo:lo + POOL_GDIM, lo:lo + POOL_GDIM].set(w_pool[gi])
    wo_gla = jnp.pad(w_out[:nv].reshape(GLA_HEADS, GLA_DV, -1), ((0, 0), (0, pad), (0, 0))).reshape(
        GLA_HEADS * GLA_DVP, -1)
    return dict(win=win.astype(BF16), wg=wg.astype(BF16), bg=b_g[None], go=go, wpool=wpool.astype(BF16),
                pscale=pool_scale[None], wo_gla=wo_gla.astype(BF16), wo_pool=w_out[nv:].astype(BF16))


def _odd_proj_body(x_ref, gmix_ref, win_ref, wg_ref, bg_ref, q_ref, k_ref, v_ref, r_ref, u_ref, g_ref):
    xn = _rms(x_ref[...], gmix_ref[...])
    z = _dot(xn, win_ref[...])
    nk = GLA_HEADS * GLA_DK
    nvp = GLA_HEADS * GLA_DVP
    q_ref[...] = z[:, :nk] * (GLA_DK ** -0.5)
    k_ref[...] = z[:, nk:2 * nk]
    v_ref[...] = z[:, 2 * nk:2 * nk + nvp]
    r_ref[...] = z[:, 2 * nk + nvp:2 * nk + 2 * nvp]
    o_u = 2 * nk + 2 * nvp
    u_ref[...] = z[:, o_u:o_u + POOL_DIM]
    zg = _dot(z[:, o_u + POOL_DIM:o_u + POOL_DIM + LANE], wg_ref[...]) + bg_ref[...]
    g_ref[...] = (jnp.minimum(zg, 0.0) - jnp.log1p(jnp.exp(-jnp.abs(zg)))) * (1.0 / GLA_TAU)


def _odd_proj(x, g_mix, ow, tm):
    m, d = x.shape
    nk = GLA_HEADS * GLA_DK
    nvp = GLA_HEADS * GLA_DVP
    row = lambda w: pl.BlockSpec((tm, w), lambda i: (i, 0))
    widths = [nk, nk, nvp, nvp, POOL_DIM, nk]
    return pl.pallas_call(
        _odd_proj_body, grid=(m // tm,),
        in_specs=[row(d), _const_spec((1, d)), _const_spec(ow["win"].shape), _const_spec(ow["wg"].shape),
                  _const_spec((1, nk))],
        out_specs=[row(w) for w in widths],
        out_shape=[jax.ShapeDtypeStruct((m, w), F32) for w in widths],
        compiler_params=_cparams("parallel"), name="odd_proj",
    )(x, g_mix[None], ow["win"], ow["wg"], ow["bg"])


def _gla_out(o, r, go):
    outs = []
    for h in range(GLA_HEADS):
        sl = slice(h * GLA_DVP, (h + 1) * GLA_DVP)
        oh = o[h] if isinstance(o, (list, tuple)) else o[:, sl]
        ms = jnp.sum(oh * oh, axis=-1, keepdims=True) * (1.0 / GLA_DV)
        rh = r[:, sl]
        outs.append(oh * lax.rsqrt(ms + EPS) * go[:, sl] * (rh / (1.0 + jnp.exp(-rh))))
    return jnp.concatenate(outs, axis=-1)


def _gla_scan_body(q_ref, k_ref, v_ref, g_ref, r_ref, go_ref, o_ref, st_ref, st_sc):
    c = pl.program_id(1)

    @pl.when(c == 0)
    def _():
        st_sc[...] = jnp.zeros(st_sc.shape, F32)

    cs = q_ref.shape[1]
    row = lax.broadcasted_iota(jnp.int32, (cs, cs), 0)
    col = lax.broadcasted_iota(jnp.int32, (cs, cs), 1)
    tril = (col <= row)
    tril_bf = tril.astype(BF16)
    outs = []
    for h in range(GLA_HEADS):
        ks = slice(h * GLA_DK, (h + 1) * GLA_DK)
        vs = slice(h * GLA_DVP, (h + 1) * GLA_DVP)
        q, k, v, g = q_ref[0, :, ks], k_ref[0, :, ks], v_ref[0, :, vs], g_ref[0, :, ks]
        g_hi = g.astype(BF16)
        g_r = g - g_hi.astype(F32)
        g_mid = g_r.astype(BF16)
        g_lo = (g_r - g_mid.astype(F32)).astype(BF16)
        gc = (jnp.dot(tril_bf, g_hi, preferred_element_type=F32)
              + jnp.dot(tril_bf, g_mid, preferred_element_type=F32)
              + jnp.dot(tril_bf, g_lo, preferred_element_type=F32))
        g_last = gc[cs - 1:cs, :]
        qd = q * jnp.exp(gc)
        a = jnp.where(tril, _dot_nt(qd, k * jnp.exp(-gc)), 0.0)
        st = st_sc[h]
        outs.append(_dot_nt(qd, st) + _dot(a, v))
        st_sc[h] = st * jnp.exp(g_last) + _dot_tn(v, k * jnp.exp(g_last - gc))
    o_ref[0] = _gla_out(outs, r_ref[0], go_ref[...])

    @pl.when(c == pl.num_programs(1) - 1)
    def _():
        st_ref[0] = st_sc[...]


def _gla_scan(q, k, v, g, r, go):
    n, s, nk = q.shape
    nvp = v.shape[2]
    cs = GLA_CHUNK
    blk = lambda w: pl.BlockSpec((1, cs, w), lambda b, c: (b, c, 0))
    return pl.pallas_call(
        _gla_scan_body, grid=(n, s // cs),
        in_specs=[blk(nk), blk(nk), blk(nvp), blk(nk), blk(nvp), _const_spec((1, nvp))],
        out_specs=[blk(nvp), pl.BlockSpec((1, GLA_HEADS, GLA_DVP, GLA_DK), lambda b, c: (b, 0, 0, 0))],
        out_shape=[jax.ShapeDtypeStruct((n, s, nvp), F32),
                   jax.ShapeDtypeStruct((n, GLA_HEADS, GLA_DVP, GLA_DK), F32)],
        scratch_shapes=[pltpu.VMEM((GLA_HEADS, GLA_DVP, GLA_DK), F32)],
        compiler_params=_cparams("parallel", "arbitrary"), name="gla_scan",
    )(q, k, v, g, r, go)


def _pool_group_select(vals):
    grp = lax.broadcasted_iota(jnp.int32, (1, POOL_DIM), 1) // POOL_GDIM
    out = vals[-1]
    for gi in range(len(vals) - 2, -1, -1):
        out = jnp.where(grp == gi, vals[gi], out)
    return out


def _pool_body(u_ref, wp_ref, ps_ref, o_ref, ext_sc, *, tm):
    i = pl.program_id(1)
    hist = POOL_HIST + 1

    @pl.when(i == 0)
    def _():
        ext_sc[0:hist, :] = jnp.zeros((hist, POOL_DIM), F32)

    u = u_ref[0]
    ext_sc[hist:hist + tm, :] = u
    acc = u
    sums = []
    for k in range(1, max(POOL_WINDOWS)):
        acc = acc + ext_sc[hist - k:hist - k + tm, :]
        if k + 1 in POOL_WINDOWS:
            sums.append(acc)
    pos = i * tm + lax.broadcasted_iota(jnp.int32, (tm, 1), 0)
    win = _pool_group_select([jnp.full((1, POOL_DIM), w, jnp.int32) for w in POOL_WINDOWS])
    cnt = jnp.minimum(pos + 1, win).astype(F32)
    diff = _pool_group_select(sums) / cnt - u
    o_ref[0] = _dot(diff, wp_ref[...]) * ps_ref[...]
    ext_sc[0:hist, :] = ext_sc[tm:tm + hist, :]


def _pool_prompt(u, wpool, pscale, tm=512):
    n, s, pd = u.shape
    blk = pl.BlockSpec((1, tm, pd), lambda b, i: (b, i, 0))
    return pl.pallas_call(
        functools.partial(_pool_body, tm=tm), grid=(n, s // tm),
        in_specs=[blk, _const_spec((pd, pd)), _const_spec((1, pd))],
        out_specs=blk, out_shape=jax.ShapeDtypeStruct((n, s, pd), F32),
        scratch_shapes=[pltpu.VMEM((tm + POOL_HIST + 1, pd), F32)],
        compiler_params=_cparams("parallel", "arbitrary"), name="pool_prompt",
    )(u, wpool, pscale)


def _pool_sample_body(ue_ref, wp_ref, ps_ref, o_ref, *, past):
    ue = ue_ref[...]
    t = ue.shape[1]
    rowi = lax.broadcasted_iota(jnp.int32, (1, t, 1), 1)
    sums = [jnp.sum(jnp.where(rowi >= t - w, ue, 0.0), axis=1) for w in POOL_WINDOWS]
    cnt = _pool_group_select(
        [jnp.full((1, POOL_DIM), float(min(past + 1, w)), F32) for w in POOL_WINDOWS])
    diff = _pool_group_select(sums) / cnt - ue[:, t - 1, :]
    o_ref[...] = _dot(diff, wp_ref[...]) * ps_ref[...]


def _pool_sample(u_ext, wpool, pscale, past):
    b, t, pd = u_ext.shape
    return pl.pallas_call(
        functools.partial(_pool_sample_body, past=past), grid=(1,),
        in_specs=[_const_spec((b, t, pd)), _const_spec((pd, pd)), _const_spec((1, pd))],
        out_specs=_const_spec((b, pd)), out_shape=jax.ShapeDtypeStruct((b, pd), F32),
        compiler_params=_cparams("arbitrary"), name="pool_sample",
    )(u_ext, wpool, pscale)


def _gla_sample_body(q_ref, k_ref, v_ref, g_ref, r_ref, go_ref, st_ref, o_ref, sn_ref, o_sc, *, bs):
    eye = (lax.broadcasted_iota(jnp.int32, (GLA_DK, GLA_DK), 0)
           == lax.broadcasted_iota(jnp.int32, (GLA_DK, GLA_DK), 1))

    def col(rowv):
        return jnp.sum(jnp.where(eye, rowv, 0.0), axis=1, keepdims=True)

    o_sc[...] = jnp.zeros(o_sc.shape, F32)
    for b in range(bs):
        for h in range(GLA_HEADS):
            ks = slice(h * GLA_DK, (h + 1) * GLA_DK)
            q = q_ref[b:b + 1, ks]
            k = k_ref[b:b + 1, ks]
            eg = jnp.exp(g_ref[b:b + 1, ks])
            v = v_ref[b:b + 1, h * GLA_DVP:h * GLA_DVP + GLA_DV]
            s_new = st_ref[b, h] * col(eg) + col(k) * v
            sn_ref[b, h] = s_new
            o_sc[b:b + 1, h * GLA_DVP:h * GLA_DVP + GLA_DV] = jnp.sum(col(q) * s_new, axis=0, keepdims=True)
    o_ref[...] = _gla_out(o_sc[...], r_ref[...], go_ref[...])


def _gla_sample(q, k, v, g, r, go, state, bs=8):
    b, nk = q.shape
    nvp = v.shape[1]
    row = lambda w: pl.BlockSpec((bs, w), lambda i: (i, 0))
    st = pl.BlockSpec((bs, GLA_HEADS, GLA_DK, GLA_DV), lambda i: (i, 0, 0, 0))
    return pl.pallas_call(
        functools.partial(_gla_sample_body, bs=bs), grid=(b // bs,),
        in_specs=[row(nk), row(nk), row(nvp), row(nk), row(nvp), _const_spec((1, nvp)), st],
        out_specs=[row(nvp), st],
        out_shape=[jax.ShapeDtypeStruct((b, nvp), F32), jax.ShapeDtypeStruct(state.shape, F32)],
        scratch_shapes=[pltpu.VMEM((bs, nvp), F32)],
        compiler_params=_cparams("parallel"), name="gla_sample",
    )(q, k, v, g, r, go, state)


def _sample_q_body(qa_ref, gak_ref, wk_ref, selr_ref, selsw_ref, qabs_ref, qr_ref, qsw_ref):
    qa = qa_ref[...]
    lane = lax.broadcasted_iota(jnp.int32, (1, qa.shape[1]), 1)
    qg = qa.astype(F32) * gak_ref[...]
    for h in range(MLA_HEADS):
        base = MLA_PAIR * (h // 2) + MLA_NOPE * (h % 2)
        nope = (lane >= base) & (lane < base + MLA_NOPE)
        qabs_ref[h] = _dot_nt(jnp.where(nope, qg, 0.0), wk_ref[...]).astype(BF16)
        qr_ref[h] = jnp.dot(qa, selr_ref[h], preferred_element_type=F32).astype(BF16)
        qsw_ref[h] = jnp.dot(qa, selsw_ref[h], preferred_element_type=F32).astype(BF16)


def _sample_q(qa, gak, wk):
    b, width = qa.shape
    cols = _mla_cols()
    selr = np.zeros((MLA_HEADS, width, MLA_ROPE), np.float32)
    selsw = np.zeros((MLA_HEADS, width, MLA_ROPE), np.float32)
    half = MLA_ROPE // 2
    for h in range(MLA_HEADS):
        rc = cols[h, MLA_NOPE:]
        selr[h, rc, np.arange(MLA_ROPE)] = 1.0
        selsw[h, rc[half:], np.arange(half)] = 1.0
        selsw[h, rc[:half], half + np.arange(half)] = -1.0
    outs = pl.pallas_call(
        _sample_q_body, grid=(1,),
        in_specs=[_const_spec(qa.shape), _const_spec(gak.shape), _const_spec(wk.shape),
                  _const_spec(selr.shape), _const_spec(selsw.shape)],
        out_specs=[_const_spec((MLA_HEADS, b, MLA_KV_LORA)), _const_spec((MLA_HEADS, b, MLA_ROPE)),
                   _const_spec((MLA_HEADS, b, MLA_ROPE))],
        out_shape=[jax.ShapeDtypeStruct((MLA_HEADS, b, MLA_KV_LORA), BF16),
                   jax.ShapeDtypeStruct((MLA_HEADS, b, MLA_ROPE), BF16),
                   jax.ShapeDtypeStruct((MLA_HEADS, b, MLA_ROPE), BF16)],
        compiler_params=_cparams("arbitrary"), name="sample_q",
    )(qa, gak, wk, jnp.asarray(selr, BF16), jnp.asarray(selsw, BF16))
    return [jnp.swapaxes(o, 0, 1) for o in outs]


def _paged_body(pt_ref, *refs, npg):
    del pt_ref
    pages = refs[:4 * npg]
    (qabs_ref, qr_ref, qsw_ref, qb_ref, ckvn_ref, krn_ref, tc_ref, ts_ref, tcn_ref, tsn_ref,
     wuk_ref, e64t_ref) = refs[4 * npg:4 * npg + 12]
    olat_ref, part_ref = refs[4 * npg + 12:4 * npg + 14]
    m_sc, l_sc, acc_sc, qcol_sc = refs[4 * npg + 14:]
    c = pl.program_id(1)
    rows = MOBA_HEADS
    eye = (lax.broadcasted_iota(jnp.int32, (MOBA_HD, LANE), 0)
           == lax.broadcasted_iota(jnp.int32, (MOBA_HD, LANE), 1))

    @pl.when(c == 0)
    def _():
        m_sc[...] = jnp.full(m_sc.shape, NEG_INF, F32)
        l_sc[...] = jnp.zeros(l_sc.shape, F32)
        acc_sc[...] = jnp.zeros(acc_sc.shape, F32)
        for h in range(MOBA_HEADS):
            qcol_sc[h] = jnp.broadcast_to(
                jnp.sum(jnp.where(eye, qb_ref[0, h:h + 1, :], 0.0), axis=1, keepdims=True), (MOBA_HD, LANE))

    qabs, qr, qsw = qabs_ref[0], qr_ref[0], qsw_ref[0]

    def mla_scores(ckv, kr_t, tc_t, ts_t):
        ckv_b = ckv.astype(BF16)
        kn = jnp.dot(ckv_b, wuk_ref[...], preferred_element_type=F32)
        ss = _dot_nt(e64t_ref[...], kn * kn) + jnp.sum(kr_t * kr_t, axis=0, keepdims=True)
        s = _dot_nt(qabs, ckv_b) + _dot(qr, kr_t * tc_t) + _dot(qsw, kr_t * ts_t)
        return s * lax.rsqrt(ss * (1.0 / MLA_QK) + EPS), ckv_b

    def softmax_update(s_list, ckv_list):
        m_prev = m_sc[...]
        m_new = m_prev
        for s in s_list:
            m_new = jnp.maximum(m_new, jnp.max(s, axis=-1, keepdims=True))
        alpha = jnp.exp(m_prev - m_new)
        l_new = alpha * l_sc[...]
        acc = alpha * acc_sc[...]
        for s, ckv_b in zip(s_list, ckv_list):
            p = jnp.exp(s - m_new)
            l_new = l_new + jnp.sum(p, axis=-1, keepdims=True)
            acc = acc + jnp.dot(p.astype(BF16), ckv_b, preferred_element_type=F32)
        m_sc[...] = m_new
        l_sc[...] = l_new
        acc_sc[...] = acc

    s_list, ckv_list = [], []
    for j in range(npg):
        sl = slice(j * PAGE_SIZE, (j + 1) * PAGE_SIZE)
        s, ckv_b = mla_scores(pages[4 * j][...], pages[4 * j + 1][...], tc_ref[:, sl], ts_ref[:, sl])
        s_list.append(s)
        ckv_list.append(ckv_b)
    softmax_update(s_list, ckv_list)

    rowi = lax.broadcasted_iota(jnp.int32, (rows, LANE), 0)
    lanei = lax.broadcasted_iota(jnp.int32, (rows, LANE), 1)
    ppb = MOBA_BLOCK // PAGE_SIZE
    for g in range(npg // ppb):
        pg = [g * ppb + t for t in range(ppb)]
        raw = []
        for j in pg:
            s8 = jnp.zeros((rows, LANE), F32)
            for h in range(MOBA_HEADS):
                sh = jnp.sum(pages[4 * j + 2][h] * qcol_sc[h], axis=0, keepdims=True)
                s8 = jnp.where(rowi == h, sh, s8)
            raw.append(s8)
        gate = sum(jnp.sum(r, axis=-1, keepdims=True) for r in raw) * (1.0 / MOBA_BLOCK)
        sc = [r * (MOBA_HD ** -0.5) for r in raw]
        m = functools.reduce(jnp.maximum, [jnp.max(t, axis=-1, keepdims=True) for t in sc])
        p = [jnp.exp(t - m) for t in sc]
        l = sum(jnp.sum(t, axis=-1, keepdims=True) for t in p)
        o8 = jnp.zeros((rows, LANE), F32)
        for h in range(MOBA_HEADS):
            pv = sum(p[t][h:h + 1, :] * pages[4 * j + 3][h] for t, j in enumerate(pg))
            ocol = jnp.sum(pv, axis=1, keepdims=True)
            o8 = jnp.where(rowi == h, jnp.sum(jnp.where(eye, ocol, 0.0), axis=0, keepdims=True), o8)
        part_ref[0, g] = jnp.where(lanei == MOBA_HD, m,
                                   jnp.where(lanei == MOBA_HD + 1, l,
                                             jnp.where(lanei == MOBA_HD + 2, gate, o8)))

    @pl.when(c == pl.num_programs(1) - 1)
    def _():
        s, ckv_b = mla_scores(ckvn_ref[0], krn_ref[0], tcn_ref[...], tsn_ref[...])
        s = jnp.where(lax.broadcasted_iota(jnp.int32, s.shape, 1) < 1, s, NEG_INF)
        softmax_update([s], [ckv_b])
        olat_ref[0] = acc_sc[...] / l_sc[...]


def _paged_attend(page_table, layer, c_ckv, c_kr, c_k, c_v, qabs, qr, qsw, qb, ckv_new, kr_new, gk_rope,
                  wuk, npg=16):
    b, n_pages = page_table.shape
    past = n_pages * PAGE_SIZE
    rows = MOBA_HEADS
    nstep = n_pages // npg
    ppb = MOBA_BLOCK // PAGE_SIZE
    c, s = _rope_tables(jnp.arange(past + 1), MLA_ROPE // 2, MLA_THETA)
    tc = (jnp.concatenate([c, c], -1) * gk_rope[None]).T
    ts = (jnp.concatenate([s, s], -1) * gk_rope[None]).T
    e64t = np.zeros((rows, MLA_HEADS * MLA_NOPE), np.float32)
    for h in range(MLA_HEADS):
        e64t[h, h * MLA_NOPE:(h + 1) * MLA_NOPE] = 1.0

    def page_spec(shape, j):
        nd = len(shape)
        return pl.BlockSpec((None, None) + shape,
                            lambda i, cc, pt, j=j: (layer, pt[i, cc * npg + j]) + (0,) * nd)

    page_specs, page_args = [], []
    for j in range(npg):
        page_specs += [page_spec((PAGE_SIZE, MLA_KV_LORA), j), page_spec((MLA_ROPE, PAGE_SIZE), j),
                       page_spec((MOBA_HEADS, MOBA_HD, PAGE_SIZE), j),
                       page_spec((MOBA_HEADS, MOBA_HD, PAGE_SIZE), j)]
        page_args += [c_ckv, c_kr, c_k, c_v]
    per_sample = lambda r, w: pl.BlockSpec((1, r, w), lambda i, cc, pt: (i, 0, 0))
    const = lambda shape: pl.BlockSpec(shape, lambda i, cc, pt: (0,) * len(shape))
    tab = pl.BlockSpec((MLA_ROPE, npg * PAGE_SIZE), lambda i, cc, pt: (0, cc))
    in_specs = page_specs + [
        per_sample(rows, MLA_KV_LORA), per_sample(rows, MLA_ROPE), per_sample(rows, MLA_ROPE),
        per_sample(rows, LANE), per_sample(rows, MLA_KV_LORA), per_sample(MLA_ROPE, rows),
        tab, tab, const((MLA_ROPE, rows)), const((MLA_ROPE, rows)), const(wuk.shape), const(e64t.shape)]
    grid_spec = pltpu.PrefetchScalarGridSpec(
        num_scalar_prefetch=1, grid=(b, nstep), in_specs=in_specs,
        out_specs=[per_sample(rows, MLA_KV_LORA),
                   pl.BlockSpec((1, npg // ppb, rows, LANE), lambda i, cc, pt: (i, cc, 0, 0))],
        scratch_shapes=[pltpu.VMEM((rows, 1), F32), pltpu.VMEM((rows, 1), F32),
                        pltpu.VMEM((rows, MLA_KV_LORA), F32),
                        pltpu.VMEM((MOBA_HEADS, MOBA_HD, LANE), F32)])
    pad_rows = lambda t: jnp.pad(t[:, None, :], ((0, 0), (0, rows - 1), (0, 0)))
    pad_cols = lambda t: jnp.pad(t[:, :, None], ((0, 0), (0, 0), (0, rows - 1)))
    tnew = lambda t: jnp.broadcast_to(t[:, past:past + 1], (MLA_ROPE, rows))
    qb_pad = jnp.pad(qb.reshape(b, MOBA_HEADS, MOBA_HD), ((0, 0), (0, 0), (0, LANE - MOBA_HD)))
    return pl.pallas_call(
        functools.partial(_paged_body, npg=npg), grid_spec=grid_spec,
        out_shape=[jax.ShapeDtypeStruct((b, rows, MLA_KV_LORA), F32),
                   jax.ShapeDtypeStruct((b, n_pages // ppb, rows, LANE), F32)],
        compiler_params=_cparams("parallel", "arbitrary"), name="paged_attend",
    )(page_table, *page_args, qabs, qr, qsw, qb_pad, pad_rows(ckv_new), pad_cols(kr_new),
      tc[:, :past], ts[:, :past], tnew(tc), tnew(ts), wuk, jnp.asarray(e64t, BF16))


def _moba_combine_body(part_ref, q_ref, kn_ref, vn_ref, out_ref):
    part = part_ref[0]
    o = part[:, :, :MOBA_HD]
    m = part[:, :, MOBA_HD:MOBA_HD + 1]
    l = part[:, :, MOBA_HD + 1:MOBA_HD + 2]
    gate = part[:, :, MOBA_HD + 2:MOBA_HD + 3]
    sel = _top_blocks(gate, jnp.full(gate.shape, True), 0)
    s_own = jnp.sum(q_ref[0] * kn_ref[0], axis=-1, keepdims=True) * (MOBA_HD ** -0.5)
    m_sel = jnp.where(sel, m, NEG_INF)
    mt = jnp.maximum(jnp.max(m_sel, axis=0), s_own)
    w = jnp.exp(m_sel - mt)
    w_own = jnp.exp(s_own - mt)
    den = jnp.sum(w * l, axis=0) + w_own
    num = jnp.sum(w * o, axis=0) + w_own * vn_ref[0]
    out_ref[0] = num / den


def _moba_combine(part, qb, kb_new, vb_new):
    b, nb = part.shape[:2]
    hb = MOBA_HEADS * MOBA_HD
    r3 = lambda t: t.reshape(b, MOBA_HEADS, MOBA_HD)
    blk = pl.BlockSpec((1, MOBA_HEADS, MOBA_HD), lambda i: (i, 0, 0))
    out = pl.pallas_call(
        _moba_combine_body, grid=(b,),
        in_specs=[pl.BlockSpec((1, nb, MOBA_HEADS, LANE), lambda i: (i, 0, 0, 0)), blk, blk, blk],
        out_specs=blk, out_shape=jax.ShapeDtypeStruct((b, MOBA_HEADS, MOBA_HD), F32),
        compiler_params=_cparams("parallel"), name="moba_combine",
    )(part, r3(qb), r3(kb_new), r3(vb_new))
    return out.reshape(b, hb)


def _latent_out_body(olat_ref, wv_ref, o_ref):
    lane = lax.broadcasted_iota(jnp.int32, (1, MLA_HEADS * MLA_V), 1)
    wv = wv_ref[...]
    acc = jnp.zeros(o_ref.shape, F32)
    for h in range(MLA_HEADS):
        wh = jnp.where((lane >= h * MLA_V) & (lane < (h + 1) * MLA_V), wv, jnp.zeros_like(wv))
        acc = acc + _dot(olat_ref[h], wh)
    o_ref[...] = acc


def _latent_out(olat_hm, wv):
    _, b, _ = olat_hm.shape
    return pl.pallas_call(
        _latent_out_body, grid=(1,),
        in_specs=[_const_spec(olat_hm.shape), _const_spec(wv.shape)],
        out_specs=_const_spec((b, MLA_HEADS * MLA_V)),
        out_shape=jax.ShapeDtypeStruct((b, MLA_HEADS * MLA_V), F32),
        compiler_params=_cparams("arbitrary"), name="latent_out",
    )(olat_hm, wv)


def _even_layer(yp, ys, page_table, layer, c_ckv, c_kr, c_k, c_v, g_mix, w_in, g_cq, g_ckv, w_uq, w_uk, w_uv,
                g_aq, g_ak, g_bq, g_bk, w_out):
    n, s, d = yp.shape
    b = ys.shape[0]
    past = page_table.shape[1] * PAGE_SIZE
    hb = MOBA_HEADS * MOBA_HD
    ew = _even_weights(w_in, g_cq, g_ckv, w_uq, w_uk, w_uv, g_aq, g_ak, g_bq, g_bk)
    wo_a = w_out[:MLA_HEADS * MLA_V].astype(BF16)
    wo_b = w_out[MLA_HEADS * MLA_V:].astype(BF16)

    ckv, kr, qa, ka, va, qb, kb, vb, km = _even_proj(yp.reshape(n * s, d), g_mix, ew, jnp.arange(s),
                                                     MOBA_BLOCK)
    r3 = lambda t: t.reshape(n, s, t.shape[-1])
    oa = _mla_attn(r3(qa), r3(ka), r3(va))
    ob = _moba_attn(r3(qb), r3(kb), r3(vb), km.reshape(n, s // MOBA_BLOCK, hb))
    yp_new = _mm2_res(oa.reshape(n * s, -1), ob.reshape(n * s, -1), wo_a, wo_b, yp.reshape(n * s, d),
                      512).reshape(n, s, d)
    p_caches = (r3(ckv), r3(kr), kb.reshape(n, s, MOBA_HEADS, MOBA_HD), vb.reshape(n, s, MOBA_HEADS, MOBA_HD))

    sckv, skr, sqa, _, _, sqb, skb, svb, _ = _even_proj(
        ys.reshape(b, d), g_mix, ew, jnp.full((b,), past, jnp.int32), b)
    qabs, qr, qsw = _sample_q(sqa, ew["gak"], ew["wk"])
    wuk = w_uk.reshape(MLA_KV_LORA, -1).astype(BF16)
    olat, part = _paged_attend(page_table, layer, c_ckv, c_kr, c_k, c_v, qabs, qr, qsw, sqb, sckv, skr,
                               g_ak[MLA_NOPE:], wuk)
    soa = _latent_out(jnp.swapaxes(olat, 0, 1), ew["wv"])
    sob = _moba_combine(part, sqb, skb, svb)
    ys_new = _mm2_res(soa, sob, wo_a, wo_b, ys.reshape(b, d), b).reshape(b, 1, d)
    s_caches = (sckv[:, None], skr[:, None], skb.reshape(b, 1, MOBA_HEADS, MOBA_HD),
                svb.reshape(b, 1, MOBA_HEADS, MOBA_HD))
    return yp_new, ys_new, p_caches, s_caches


def _odd_layer(yp, ys, past, s_prev, u_prev, g_mix, w_in, w_g, b_g, g_o, w_pool, pool_scale, w_out):
    n, s, d = yp.shape
    b = ys.shape[0]
    ow = _odd_weights(w_in, w_g, b_g, g_o, w_pool, pool_scale, w_out)

    q, k, v, r, u, g = _odd_proj(yp.reshape(n * s, d), g_mix, ow, 512)
    r3 = lambda t: t.reshape(n, s, t.shape[-1])
    og, st_t = _gla_scan(r3(q), r3(k), r3(v), r3(g), r3(r), ow["go"])
    u3 = r3(u)
    pm = _pool_prompt(u3, ow["wpool"], ow["pscale"])
    yp_new = _mm2_res(og.reshape(n * s, -1), pm.reshape(n * s, -1), ow["wo_gla"], ow["wo_pool"],
                      yp.reshape(n * s, d), 512).reshape(n, s, d)
    p_state = jnp.swapaxes(st_t, 2, 3)[..., :GLA_DV]
    p_pool = u3[:, s - POOL_HIST:]

    sq, sk, sv, sr, su, sg = _odd_proj(ys.reshape(b, d), g_mix, ow, b)
    sog, s_new = _gla_sample(sq, sk, sv, sg, sr, ow["go"], s_prev)
    u_ext = jnp.concatenate([u_prev, su[:, None]], axis=1)
    spm = _pool_sample(u_ext, ow["wpool"], ow["pscale"], past)
    ys_new = _mm2_res(sog, spm, ow["wo_gla"], ow["wo_pool"], ys.reshape(b, d), b).reshape(b, 1, d)
    return yp_new, ys_new, (p_state, p_pool), (s_new, u_ext[:, 1:])


def kernel(x_prompt, x_sample, cache_mla_ckv, cache_mla_krope, cache_moba_k, cache_moba_v, state_gla, state_pool, cache_mem_k, cache_mem_v, page_table, mem_prompt, g_mix, g_memx, g_memm, w_mq, w_mk, w_mv, g_mq, g_mk, w_mo, g_ffn, w_ff1, w_ff2, ev_w_in, ev_g_cq, ev_g_ckv, ev_w_uq, ev_w_uk, ev_w_uv, ev_g_mla_q, ev_g_mla_k, ev_g_moba_q, ev_g_moba_k, ev_w_out, od_w_in, od_w_gate, od_b_gate, od_g_gla, od_w_pool, od_pool_scale, od_w_out):
    depth = g_mix.shape[0]
    n, s, d = x_prompt.shape
    b = x_sample.shape[0]
    past = page_table.shape[1] * PAGE_SIZE
    yp, ys = x_prompt, x_sample
    outs = {name: [] for name in ("p_ckv", "p_kr", "p_k", "p_v", "s_ckv", "s_kr", "s_k", "s_v",
                                  "p_gla", "p_pool", "s_gla", "s_pool", "p_mk", "p_mv")}
    for l in range(depth):
        i = l // 2
        if l % 2 == 0:
            yp, ys, pc, sc = _even_layer(
                yp, ys, page_table, i, cache_mla_ckv, jnp.transpose(cache_mla_krope, (0, 1, 3, 2)),
                jnp.transpose(cache_moba_k, (0, 1, 3, 4, 2)), jnp.transpose(cache_moba_v, (0, 1, 3, 4, 2)),
                g_mix[l], ev_w_in[i], ev_g_cq[i], ev_g_ckv[i], ev_w_uq[i], ev_w_uk[i], ev_w_uv[i],
                ev_g_mla_q[i], ev_g_mla_k[i], ev_g_moba_q[i], ev_g_moba_k[i], ev_w_out[i])
            for name, val in zip(("p_ckv", "p_kr", "p_k", "p_v"), pc):
                outs[name].append(val)
            for name, val in zip(("s_ckv", "s_kr", "s_k", "s_v"), sc):
                outs[name].append(val)
        else:
            yp, ys, ps, ss = _odd_layer(
                yp, ys, past, state_gla[i], state_pool[i], g_mix[l], od_w_in[i], od_w_gate[i], od_b_gate[i],
                od_g_gla[i], od_w_pool[i], od_pool_scale[i], od_w_out[i])
            for name, val in zip(("p_gla", "p_pool"), ps):
                outs[name].append(val)
            for name, val in zip(("s_gla", "s_pool"), ss):
                outs[name].append(val)
        mk, mv = _mem_kv(mem_prompt, g_memm[l], w_mk[l], w_mv[l], g_mk[l])
        ml = mk.shape[1]
        outs["p_mk"].append(mk.reshape(n, ml, MEM_HEADS, MEM_HD))
        outs["p_mv"].append(mv.reshape(n, ml, MEM_HEADS, MEM_HD))
        yp = _mem_attn(yp, g_memx[l], w_mq[l], g_mq[l], mk, mv, w_mo[l], 512)
        ys = _mem_attn_sample(ys.reshape(b, d), g_memx[l], w_mq[l], g_mq[l], cache_mem_k, cache_mem_v, l,
                              w_mo[l]).reshape(b, 1, d)
        w1 = w_ff1[l].astype(BF16)
        w2 = w_ff2[l].astype(BF16)
        yp = _ffn(yp.reshape(n * s, d), g_ffn[l], w1, w2, 1024).reshape(n, s, d)
        ys = _ffn(ys.reshape(b, d), g_ffn[l], w1, w2, b).reshape(b, 1, d)
    st = lambda name: jnp.stack(outs[name])
    return (yp, ys, st("p_ckv"), st("p_kr"), st("p_k"), st("p_v"), st("p_gla"), st("p_pool"),
            st("p_mk"), st("p_mv"), st("s_ckv"), st("s_kr"), st("s_k"), st("s_v"), st("s_gla"), st("s_pool"))
```

```python
import functools
import math

import jax
import jax.numpy as jnp
import numpy as np
from jax import lax
from jax.experimental import pallas as pl
from jax.experimental.pallas import tpu as pltpu

F32 = jnp.float32
BF16 = jnp.bfloat16
NEG_INF = float("-inf")

EPS = 1e-6
D_MODEL = 1024
PAGE_SIZE = 128
MLA_HEADS = 8
MLA_NOPE = 64
MLA_ROPE = 32
MLA_QK = MLA_NOPE + MLA_ROPE
MLA_V = 64
MLA_Q_LORA = 384
MLA_KV_LORA = 256
MLA_THETA = 10000.0
MLA_PAIR = 256
MOBA_HEADS = 8
MOBA_HD = 64
MOBA_ROT = MOBA_HD // 4
MOBA_BLOCK = 256
MOBA_TOPK = 3
ROPE_THETA = 500000.0
GLA_HEADS = 4
GLA_DK = 128
GLA_DV = 192
GLA_DVP = 256
GLA_GATE_RANK = 16
GLA_TAU = 16.0
GLA_CHUNK = 64
POOL_WINDOWS = (2, 4, 8, 16)
POOL_GDIM = 64
POOL_DIM = POOL_GDIM * len(POOL_WINDOWS)
POOL_HIST = max(POOL_WINDOWS) - 1
MEM_HEADS = 4
MEM_HD = 128
LANE = 128
VMEM_LIMIT = 56 * 1024 * 1024


def _rms(x, g):
    return x * lax.rsqrt(jnp.mean(x * x, axis=-1, keepdims=True) + EPS) * g


def _dot(a, b):
    return jnp.dot(a.astype(BF16), b.astype(BF16), preferred_element_type=F32)


def _dot_nt(a, b):
    return lax.dot_general(a.astype(BF16), b.astype(BF16), (((1,), (1,)), ((), ())),
                           preferred_element_type=F32)


def _dot_tn(a, b):
    return lax.dot_general(a.astype(BF16), b.astype(BF16), (((0,), (0,)), ((), ())),
                           preferred_element_type=F32)


def _split(a):
    hi = a.astype(BF16)
    lo = (a - hi.astype(F32)).astype(BF16)
    return hi, lo


def _dot_hl(a, sel):
    hi, lo = _split(a)
    return (jnp.dot(hi, sel, preferred_element_type=F32)
            + jnp.dot(lo, sel, preferred_element_type=F32))


def _cparams(*sem):
    return pltpu.CompilerParams(dimension_semantics=sem, vmem_limit_bytes=VMEM_LIMIT)


def _const_spec(shape):
    nd = len(shape)
    return pl.BlockSpec(shape, lambda *_: (0,) * nd)


def _mla_cols():
    cols = np.zeros((MLA_HEADS, MLA_QK), np.int32)
    for h in range(MLA_HEADS):
        base = MLA_PAIR * (h // 2)
        cols[h, :MLA_NOPE] = base + MLA_NOPE * (h % 2) + np.arange(MLA_NOPE)
        cols[h, MLA_NOPE:] = base + 2 * MLA_NOPE + MLA_ROPE * (h % 2) + np.arange(MLA_ROPE)
    return cols


def _rope_tables(pos, half, theta):
    inv = jnp.exp(jnp.arange(half, dtype=F32) * (-math.log(theta) / half))
    ang = pos.astype(F32)[:, None] * inv[None, :]
    return jnp.cos(ang), jnp.sin(ang)


def _mla_pair_tables(pos):
    c, s = _rope_tables(pos, MLA_ROPE // 2, MLA_THETA)
    p = pos.shape[0]
    one = jnp.ones((p, 2 * MLA_NOPE), F32)
    pad1 = jnp.ones((p, MLA_PAIR - 2 * MLA_NOPE - 2 * MLA_ROPE), F32)
    ct = jnp.concatenate([one, c, c, c, c, pad1], -1)
    st = jnp.concatenate([0 * one, s, s, s, s, 0 * pad1], -1)
    return ct, st


def _moba_pair_tables(pos):
    c, s = _rope_tables(pos, MOBA_ROT // 2, ROPE_THETA)
    p = pos.shape[0]
    one = jnp.ones((p, MOBA_HD - MOBA_ROT), F32)
    ch = jnp.concatenate([c, c, one], -1)
    sh = jnp.concatenate([s, s, 0 * one], -1)
    return jnp.concatenate([ch, ch], -1), jnp.concatenate([sh, sh], -1)


def _even_weights(w_in, g_cq, g_ckv, w_uq, w_uk, w_uv, g_aq, g_ak, g_bq, g_bk):
    i0 = MLA_Q_LORA + MLA_KV_LORA
    i1 = i0 + MLA_ROPE
    idx = np.concatenate([np.arange(0, i0), np.arange(i1, w_in.shape[1]), np.arange(i0, i1)])
    win = jnp.pad(w_in[:, idx], ((0, 0), (0, LANE - MLA_ROPE))).astype(BF16)
    cols = _mla_cols()
    width = MLA_PAIR * MLA_HEADS // 2
    wq = jnp.zeros((MLA_Q_LORA, width), F32).at[:, cols.reshape(-1)].set(
        w_uq.reshape(MLA_Q_LORA, -1)).astype(BF16)
    wk = jnp.zeros((MLA_KV_LORA, width), F32).at[:, cols[:, :MLA_NOPE].reshape(-1)].set(
        w_uk.reshape(MLA_KV_LORA, -1)).astype(BF16)
    wv = w_uv.reshape(MLA_KV_LORA, -1).astype(BF16)
    pk = np.zeros((MLA_ROPE, width), np.float32)
    e = np.zeros((width, LANE), np.float32)
    for h in range(MLA_HEADS):
        pk[np.arange(MLA_ROPE), cols[h, MLA_NOPE:]] = 1.0
        e[cols[h], h] = 1.0
    gaq = jnp.zeros((1, width), F32).at[0, cols.reshape(-1)].set(jnp.tile(g_aq, MLA_HEADS))
    gak = jnp.zeros((1, width), F32).at[0, cols.reshape(-1)].set(jnp.tile(g_ak, MLA_HEADS))
    b64 = np.kron(np.eye(MOBA_HEADS, dtype=np.float32), np.ones((MOBA_HD, MOBA_HD), np.float32))
    return dict(
        win=win, gcq=g_cq[None], gckv=g_ckv[None], wq=wq, wk=wk, wv=wv,
        pk=jnp.asarray(pk, BF16), e=jnp.asarray(e, BF16), et=jnp.asarray(e.T, BF16),
        gaq=gaq, gak=gak, b64=jnp.asarray(b64, BF16),
        gbq=jnp.tile(g_bq, MOBA_HEADS)[None], gbk=jnp.tile(g_bk, MOBA_HEADS)[None])


def _even_proj_body(x_ref, gmix_ref, win_ref, gcq_ref, gckv_ref, wq_ref, wk_ref, wv_ref, pk_ref,
                    e_ref, et_ref, gaq_ref, gak_ref, ca_ref, sa_ref, b64_ref, gbq_ref, gbk_ref,
                    cb_ref, sb_ref,
                    ckv_ref, kr_ref, qa_ref, ka_ref, va_ref, qb_ref, kb_ref, vb_ref, km_ref):
    xn = _rms(x_ref[...], gmix_ref[...])
    z = _dot(xn, win_ref[...])
    o_ckv = MLA_Q_LORA
    o_qb = o_ckv + MLA_KV_LORA
    hb = MOBA_HEADS * MOBA_HD
    cqn = _rms(z[:, :o_ckv], gcq_ref[...])
    ckv = _rms(z[:, o_ckv:o_qb], gckv_ref[...])
    qb = z[:, o_qb:o_qb + hb]
    kb = z[:, o_qb + hb:o_qb + 2 * hb]
    vb = z[:, o_qb + 2 * hb:o_qb + 3 * hb]
    kr = z[:, o_qb + 3 * hb:o_qb + 3 * hb + MLA_ROPE]
    ckv_ref[...] = ckv
    kr_ref[...] = kr
    vb_ref[...] = vb

    npair = MLA_HEADS // 2
    ca = jnp.concatenate([ca_ref[...]] * npair, axis=-1)
    sa = jnp.concatenate([sa_ref[...]] * npair, axis=-1)
    width = MLA_PAIR * npair
    lane = lax.broadcasted_iota(jnp.int32, (1, width), 1)
    in_pair = lane % MLA_PAIR
    first_half = ((in_pair >= 2 * MLA_NOPE) & (in_pair < 2 * MLA_NOPE + 2 * MLA_ROPE)
                  & (lane % MLA_ROPE < MLA_ROPE // 2))

    def norm_rope_a(t, g):
        ss = _dot_hl(t * t, e_ref[...])
        rs = lax.rsqrt(ss * (1.0 / MLA_QK) + EPS)
        tn = t * _dot_hl(rs, et_ref[...]) * g
        sw = jnp.where(first_half, -pltpu.roll(tn, width - MLA_ROPE // 2, 1),
                       pltpu.roll(tn, MLA_ROPE // 2, 1))
        return tn * ca + sw * sa

    qa = norm_rope_a(_dot(cqn, wq_ref[...]), gaq_ref[...]) * (MLA_QK ** -0.5)
    ka = norm_rope_a(_dot(ckv, wk_ref[...]) + _dot_hl(kr, pk_ref[...]), gak_ref[...])
    qa_ref[...] = qa.astype(BF16)
    ka_ref[...] = ka.astype(BF16)
    va_ref[...] = _dot(ckv, wv_ref[...]).astype(BF16)

    cb = jnp.concatenate([cb_ref[...]] * (MOBA_HEADS // 2), axis=-1)
    sb = jnp.concatenate([sb_ref[...]] * (MOBA_HEADS // 2), axis=-1)
    lane_b = lax.broadcasted_iota(jnp.int32, (1, hb), 1)
    first_half_b = lane_b % MOBA_HD < MOBA_ROT // 2

    def norm_rope_b(t, g):
        ss = _dot_hl(t * t, b64_ref[...])
        tn = t * lax.rsqrt(ss * (1.0 / MOBA_HD) + EPS) * g
        sw = jnp.where(first_half_b, -pltpu.roll(tn, hb - MOBA_ROT // 2, 1),
                       pltpu.roll(tn, MOBA_ROT // 2, 1))
        return tn * cb + sw * sb

    qb_ref[...] = norm_rope_b(qb, gbq_ref[...])
    kbn = norm_rope_b(kb, gbk_ref[...])
    kb_ref[...] = kbn
    km_ref[0] = jnp.mean(kbn, axis=0, keepdims=True)


def _even_proj(x, g_mix, ew, pos, tm):
    m, d = x.shape
    nt = m // tm
    ca, sa = _mla_pair_tables(pos)
    cb, sb = _moba_pair_tables(pos)
    npt = pos.shape[0] // tm
    row = lambda w: pl.BlockSpec((tm, w), lambda i: (i, 0))
    tab = lambda w: pl.BlockSpec((tm, w), lambda i: (i % npt, 0))
    hb = MOBA_HEADS * MOBA_HD
    wa = MLA_PAIR * MLA_HEADS // 2
    consts = [g_mix[None], ew["win"], ew["gcq"], ew["gckv"], ew["wq"], ew["wk"], ew["wv"], ew["pk"],
              ew["e"], ew["et"], ew["gaq"], ew["gak"]]
    consts2 = [ew["b64"], ew["gbq"], ew["gbk"]]
    in_specs = ([row(d)] + [_const_spec(c.shape) for c in consts] + [tab(MLA_PAIR), tab(MLA_PAIR)]
                + [_const_spec(c.shape) for c in consts2] + [tab(LANE), tab(LANE)])
    out_shape = [
        jax.ShapeDtypeStruct((m, MLA_KV_LORA), F32), jax.ShapeDtypeStruct((m, MLA_ROPE), F32),
        jax.ShapeDtypeStruct((m, wa), BF16), jax.ShapeDtypeStruct((m, wa), BF16),
        jax.ShapeDtypeStruct((m, MLA_HEADS * MLA_V), BF16),
        jax.ShapeDtypeStruct((m, hb), F32), jax.ShapeDtypeStruct((m, hb), F32),
        jax.ShapeDtypeStruct((m, hb), F32), jax.ShapeDtypeStruct((nt, 1, hb), F32)]
    out_specs = [row(MLA_KV_LORA), row(MLA_ROPE), row(wa), row(wa), row(MLA_HEADS * MLA_V),
                 row(hb), row(hb), row(hb), pl.BlockSpec((1, 1, hb), lambda i: (i, 0, 0))]
    return pl.pallas_call(
        _even_proj_body, grid=(nt,), in_specs=in_specs, out_specs=out_specs, out_shape=out_shape,
        compiler_params=_cparams("parallel"), name="even_proj",
    )(x, *consts, ca, sa, *consts2, cb, sb)


def _mla_head_mask(e):
    lane = lax.broadcasted_iota(jnp.int32, (1, MLA_PAIR), 1)
    nope = (lane >= MLA_NOPE * e) & (lane < MLA_NOPE * (e + 1))
    r0 = 2 * MLA_NOPE + MLA_ROPE * e
    return nope | ((lane >= r0) & (lane < r0 + MLA_ROPE))


def _mla_attn_body(q_ref, k_ref, v_ref, o_ref, s_sc, mrun_sc, lrun_sc, acc_sc, *, tq):
    qi = pl.program_id(2)
    q = q_ref[0]
    qs = [jnp.where(_mla_head_mask(e), q, jnp.zeros_like(q)) for e in (0, 1)]
    causal = (lax.broadcasted_iota(jnp.int32, (tq, tq), 1)
              <= lax.broadcasted_iota(jnp.int32, (tq, tq), 0))
    mrun_sc[...] = jnp.full(mrun_sc.shape, NEG_INF, F32)

    def scores(ki, diagonal):
        k = k_ref[0, pl.ds(pl.multiple_of(ki * tq, tq), tq), :]
        for e in (0, 1):
            s = _dot_nt(qs[e], k)
            if diagonal:
                s = jnp.where(causal, s, NEG_INF)
            s_sc[e, ki] = s
            mrun_sc[e] = jnp.maximum(mrun_sc[e], s)

    def score_step(ki, carry):
        scores(ki, False)
        return carry

    lax.fori_loop(0, qi, score_step, 0)
    scores(qi, True)
    for e in (0, 1):
        mrun_sc[e] = jnp.broadcast_to(jnp.max(mrun_sc[e], axis=-1, keepdims=True), (tq, tq))
    lrun_sc[...] = jnp.zeros(lrun_sc.shape, F32)
    acc_sc[...] = jnp.zeros(acc_sc.shape, F32)

    def value_step(ki, carry):
        v = v_ref[0, pl.ds(pl.multiple_of(ki * tq, tq), tq), :]
        for e in (0, 1):
            p = jnp.exp(s_sc[e, ki] - mrun_sc[e])
            lrun_sc[e] += p
            acc_sc[e] += _dot(p, v)
        return carry

    lax.fori_loop(0, qi + 1, value_step, 0)
    lane_o = lax.broadcasted_iota(jnp.int32, (1, 2 * MLA_V), 1)
    o0 = acc_sc[0] / jnp.sum(lrun_sc[0], axis=-1, keepdims=True)
    o1 = acc_sc[1] / jnp.sum(lrun_sc[1], axis=-1, keepdims=True)
    o_ref[0] = jnp.where(lane_o < MLA_V, o0, o1)


def _mla_attn(qa, ka, va, tq=512):
    n, s, _ = qa.shape
    npair = MLA_HEADS // 2
    return pl.pallas_call(
        functools.partial(_mla_attn_body, tq=tq),
        grid=(n, npair, s // tq),
        in_specs=[pl.BlockSpec((1, tq, MLA_PAIR), lambda b, p, i: (b, i, p)),
                  pl.BlockSpec((1, s, MLA_PAIR), lambda b, p, i: (b, 0, p)),
                  pl.BlockSpec((1, s, 2 * MLA_V), lambda b, p, i: (b, 0, p))],
        out_specs=pl.BlockSpec((1, tq, 2 * MLA_V), lambda b, p, i: (b, i, p)),
        out_shape=jax.ShapeDtypeStruct((n, s, MLA_HEADS * MLA_V), F32),
        scratch_shapes=[pltpu.VMEM((2, s // tq, tq, tq), F32), pltpu.VMEM((2, tq, tq), F32),
                        pltpu.VMEM((2, tq, tq), F32), pltpu.VMEM((2, tq, 2 * MLA_V), F32)],
        compiler_params=_cparams("parallel", "parallel", "arbitrary"), name="mla_attn",
    )(qa, ka, va)


def _top_blocks(gate, eligible, axis):
    idx = lax.broadcasted_iota(jnp.int32, gate.shape, axis)
    big = jnp.int32(1 << 20)
    g = jnp.where(eligible, gate, NEG_INF)
    sel = jnp.zeros(gate.shape, jnp.bool_)
    for _ in range(MOBA_TOPK):
        mx = jnp.max(g, axis=axis, keepdims=True)
        cand = (g == mx) & (mx > NEG_INF)
        first = jnp.min(jnp.where(cand, idx, big), axis=axis, keepdims=True)
        pick = idx == first
        sel = sel | pick
        g = jnp.where(pick, NEG_INF, g)
    return sel


def _moba_attn_body(q_ref, k_ref, vt_ref, km_ref, o_ref, s_sc, sel_sc, mrun_sc, lrun_sc, acc_sc, *, tq):
    qi = pl.program_id(2)
    q = q_ref[0]
    lane = lax.broadcasted_iota(jnp.int32, (1, 2 * MOBA_HD), 1)
    nblk = km_ref.shape[1]
    blk = lax.broadcasted_iota(jnp.int32, (nblk, tq), 0)
    km_hi, km_lo = _split(km_ref[0])
    qs = []
    for e in (0, 1):
        qe = jnp.where((lane >= MOBA_HD * e) & (lane < MOBA_HD * (e + 1)), q, 0.0)
        qs.append((qe * (MOBA_HD ** -0.5)).astype(BF16))
        q_hi, q_lo = _split(qe)
        nt = (((1,), (1,)), ((), ()))
        gate = (lax.dot_general(km_hi, q_hi, nt, preferred_element_type=F32)
                + lax.dot_general(km_lo, q_hi, nt, preferred_element_type=F32)
                + lax.dot_general(km_hi, q_lo, nt, preferred_element_type=F32))
        sel_sc[e] = _top_blocks(gate, blk < qi, 0).astype(F32)
    causal = (lax.broadcasted_iota(jnp.int32, (tq, tq), 0)
              <= lax.broadcasted_iota(jnp.int32, (tq, tq), 1))

    def scores(j, own):
        k = k_ref[0, pl.ds(pl.multiple_of(j * tq, tq), tq), :].astype(BF16)
        for e in (0, 1):
            ok = causal if own else sel_sc[e, pl.ds(j, 1), :] > 0.5
            s = jnp.where(ok, _dot_nt(k, qs[e]), NEG_INF)
            s_sc[e, j] = s
            if own:
                mrun_sc[e] = s
            else:
                mrun_sc[e] = jnp.maximum(mrun_sc[e], s)

    def score_step(j, carry):
        scores(j, False)
        return carry

    scores(qi, True)
    lax.fori_loop(0, qi, score_step, 0)
    for e in (0, 1):
        mrun_sc[e] = jnp.broadcast_to(jnp.max(mrun_sc[e], axis=0, keepdims=True), (tq, tq))
    lrun_sc[...] = jnp.zeros(lrun_sc.shape, F32)
    acc_sc[...] = jnp.zeros(acc_sc.shape, F32)

    def value_step(j, carry):
        vt = vt_ref[0, 0, j].astype(BF16)
        for e in (0, 1):
            p = jnp.exp(s_sc[e, j] - mrun_sc[e])
            lrun_sc[e] += p
            acc_sc[e] += jnp.dot(vt, p.astype(BF16), preferred_element_type=F32)
        return carry

    lax.fori_loop(0, qi + 1, value_step, 0)
    o0 = acc_sc[0] / jnp.sum(lrun_sc[0], axis=0, keepdims=True)
    o1 = acc_sc[1] / jnp.sum(lrun_sc[1], axis=0, keepdims=True)
    drow = lax.broadcasted_iota(jnp.int32, (2 * MOBA_HD, 1), 0)
    o_ref[0] = jnp.where(drow < MOBA_HD, o0, o1).T


def _moba_attn(qb, kb, vb, km):
    n, s, hb = qb.shape
    tq = MOBA_BLOCK
    nblk = s // tq
    npair = hb // LANE
    vt = vb.reshape(n, nblk, tq, npair, LANE).transpose(0, 3, 1, 4, 2)
    return pl.pallas_call(
        functools.partial(_moba_attn_body, tq=tq),
        grid=(n, npair, nblk),
        in_specs=[pl.BlockSpec((1, tq, LANE), lambda b, p, i: (b, i, p)),
                  pl.BlockSpec((1, s, LANE), lambda b, p, i: (b, 0, p)),
                  pl.BlockSpec((1, 1, nblk, LANE, tq), lambda b, p, i: (b, p, 0, 0, 0)),
                  pl.BlockSpec((1, nblk, LANE), lambda b, p, i: (b, 0, p))],
        out_specs=pl.BlockSpec((1, tq, LANE), lambda b, p, i: (b, i, p)),
        out_shape=jax.ShapeDtypeStruct((n, s, hb), F32),
        scratch_shapes=[pltpu.VMEM((2, nblk, tq, tq), F32), pltpu.VMEM((2, nblk, tq), F32),
                        pltpu.VMEM((2, tq, tq), F32), pltpu.VMEM((2, tq, tq), F32),
                        pltpu.VMEM((2, LANE, tq), F32)],
        compiler_params=_cparams("parallel", "parallel", "arbitrary"), name="moba_attn",
    )(qb, kb, vt, km)


def _mm2_res_body(a_ref, b_ref, wa_ref, wb_ref, r_ref, o_ref):
    o_ref[...] = r_ref[...] + _dot(a_ref[...], wa_ref[...]) + _dot(b_ref[...], wb_ref[...])


def _mm2_res(a, b, wa, wb, res, tm):
    m, d = res.shape
    row = lambda w: pl.BlockSpec((tm, w), lambda i: (i, 0))
    return pl.pallas_call(
        _mm2_res_body, grid=(m // tm,),
        in_specs=[row(a.shape[1]), row(b.shape[1]), _const_spec(wa.shape), _const_spec(wb.shape), row(d)],
        out_specs=row(d), out_shape=jax.ShapeDtypeStruct((m, d), F32),
        compiler_params=_cparams("parallel"), name="mix_out",
    )(a, b, wa, wb, res)


def _mem_kv_body(mem_ref, gm_ref, wk_ref, wv_ref, gk_ref, k_ref, v_ref):
    mn = _rms(mem_ref[0], gm_ref[...])
    k = _dot(mn, wk_ref[...])
    for h in range(MEM_HEADS):
        sl = slice(h * MEM_HD, (h + 1) * MEM_HD)
        k_ref[0, :, sl] = _rms(k[:, sl], gk_ref[...])
    v_ref[0] = _dot(mn, wv_ref[...])


def _mem_kv(mem, g_m, w_k, w_v, g_k):
    n, ml, d = mem.shape
    hd = MEM_HEADS * MEM_HD
    blk = lambda w: pl.BlockSpec((1, ml, w), lambda b: (b, 0, 0))
    return pl.pallas_call(
        _mem_kv_body, grid=(n,),
        in_specs=[blk(d), _const_spec((1, d)), _const_spec((d, hd)), _const_spec((d, hd)),
                  _const_spec((1, MEM_HD))],
        out_specs=[blk(hd), blk(hd)],
        out_shape=[jax.ShapeDtypeStruct((n, ml, hd), F32)] * 2,
        compiler_params=_cparams("parallel"), name="mem_kv",
    )(mem, g_m[None], w_k.reshape(d, hd).astype(BF16), w_v.reshape(d, hd).astype(BF16), g_k[None])


def _mem_attn_body(x_ref, g_ref, wq_ref, gq_ref, k_ref, v_ref, wo_ref, o_ref):
    x = x_ref[0]
    q = _dot(_rms(x, g_ref[...]), wq_ref[...])
    k = k_ref[0].astype(BF16)
    v = v_ref[0].astype(BF16)
    outs = []
    for h in range(MEM_HEADS):
        sl = slice(h * MEM_HD, (h + 1) * MEM_HD)
        qh = _rms(q[:, sl], gq_ref[...])
        s = _dot_nt(qh, k[:, sl]) * (MEM_HD ** -0.5)
        p = jnp.exp(s - jnp.max(s, axis=-1, keepdims=True))
        p = p / jnp.sum(p, axis=-1, keepdims=True)
        outs.append(_dot(p, v[:, sl]))
    o_ref[0] = x + _dot(jnp.concatenate(outs, axis=-1), wo_ref[...])


def _mem_attn(x, g, w_q, g_q, k, v, w_o, tm):
    n, s, d = x.shape
    ml, hd = k.shape[1:]
    return pl.pallas_call(
        _mem_attn_body, grid=(n, s // tm),
        in_specs=[pl.BlockSpec((1, tm, d), lambda b, i: (b, i, 0)), _const_spec((1, d)),
                  _const_spec((d, hd)), _const_spec((1, MEM_HD)),
                  pl.BlockSpec((1, ml, hd), lambda b, i: (b, 0, 0)),
                  pl.BlockSpec((1, ml, hd), lambda b, i: (b, 0, 0)), _const_spec((hd, d))],
        out_specs=pl.BlockSpec((1, tm, d), lambda b, i: (b, i, 0)),
        out_shape=jax.ShapeDtypeStruct((n, s, d), F32),
        compiler_params=_cparams("parallel", "parallel"), name="mem_attn",
    )(x, g[None], w_q.reshape(d, hd).astype(BF16), g_q[None], k, v, w_o.reshape(hd, d).astype(BF16))


def _mem_attn_sample_body(x_ref, g_ref, wq_ref, gq_ref, k_ref, v_ref, wo_ref, o_ref, q_sc, att_sc, *, bs):
    x = x_ref[...]
    q = _dot(_rms(x, g_ref[...]), wq_ref[...])
    for h in range(MEM_HEADS):
        q_sc[:, h, :] = _rms(q[:, h * MEM_HD:(h + 1) * MEM_HD], gq_ref[...]) * (MEM_HD ** -0.5)
    for b in range(bs):
        s = jnp.sum(k_ref[b] * q_sc[b], axis=-1, keepdims=True)
        p = jnp.exp(s - jnp.max(s, axis=0, keepdims=True))
        o = jnp.sum(p * v_ref[b], axis=0) / jnp.sum(p, axis=0)
        for h in range(MEM_HEADS):
            att_sc[b:b + 1, h * MEM_HD:(h + 1) * MEM_HD] = o[h:h + 1, :]
    o_ref[...] = x + _dot(att_sc[...], wo_ref[...])


def _mem_attn_sample(x, g, w_q, g_q, cache_k, cache_v, layer, w_o, bs=8):
    b, d = x.shape
    ml = cache_k.shape[2]
    hd = MEM_HEADS * MEM_HD
    mem = pl.BlockSpec((None, bs, ml, MEM_HEADS, MEM_HD), lambda i: (layer, i, 0, 0, 0))
    return pl.pallas_call(
        functools.partial(_mem_attn_sample_body, bs=bs), grid=(b // bs,),
        in_specs=[pl.BlockSpec((bs, d), lambda i: (i, 0)), _const_spec((1, d)),
                  _const_spec((d, hd)), _const_spec((1, MEM_HD)), mem, mem, _const_spec((hd, d))],
        out_specs=pl.BlockSpec((bs, d), lambda i: (i, 0)),
        out_shape=jax.ShapeDtypeStruct((b, d), F32),
        scratch_shapes=[pltpu.VMEM((bs, MEM_HEADS, MEM_HD), F32), pltpu.VMEM((bs, hd), F32)],
        compiler_params=_cparams("parallel"), name="mem_attn_sample",
    )(x, g[None], w_q.reshape(d, hd).astype(BF16), g_q[None], cache_k, cache_v,
      w_o.reshape(hd, d).astype(BF16))


def _ffn_body(x_ref, g_ref, w1_ref, w2_ref, o_ref, xn_sc, acc_sc):
    j = pl.program_id(1)

    @pl.when(j == 0)
    def _():
        xn_sc[...] = _rms(x_ref[...], g_ref[...]).astype(BF16)
        acc_sc[...] = jnp.zeros(acc_sc.shape, F32)

    h = jnp.maximum(jnp.dot(xn_sc[...], w1_ref[...], preferred_element_type=F32), 0.0)
    acc_sc[...] += _dot(h * h, w2_ref[...])

    @pl.when(j == pl.num_programs(1) - 1)
    def _():
        o_ref[...] = x_ref[...] + acc_sc[...]


def _ffn(x, g, w1, w2, tm, tf=512):
    m, d = x.shape
    ff = w1.shape[1]
    return pl.pallas_call(
        _ffn_body, grid=(m // tm, ff // tf),
        in_specs=[pl.BlockSpec((tm, d), lambda i, j: (i, 0)), _const_spec((1, d)),
                  pl.BlockSpec((d, tf), lambda i, j: (0, j)), pl.BlockSpec((tf, d), lambda i, j: (j, 0))],
        out_specs=pl.BlockSpec((tm, d), lambda i, j: (i, 0)),
        out_shape=jax.ShapeDtypeStruct((m, d), F32),
        scratch_shapes=[pltpu.VMEM((tm, d), BF16), pltpu.VMEM((tm, d), F32)],
        compiler_params=_cparams("parallel", "arbitrary"), name="ffn",
    )(x, g[None], w1, w2)


def _odd_weights(w_in, w_g, b_g, g_o, w_pool, pool_scale, w_out):
    nk = GLA_HEADS * GLA_DK
    nv = GLA_HEADS * GLA_DV
    pad = GLA_DVP - GLA_DV

    def pad_heads_cols(w):
        return jnp.pad(w.reshape(w.shape[0], GLA_HEADS, GLA_DV), ((0, 0), (0, 0), (0, pad))).reshape(
            w.shape[0], GLA_HEADS * GLA_DVP)

    o_v, o_r, o_g = 2 * nk, 2 * nk + nv, 2 * nk + 2 * nv
    o_u = o_g + GLA_GATE_RANK
    win = jnp.concatenate([
        w_in[:, :o_v], pad_heads_cols(w_in[:, o_v:o_r]), pad_heads_cols(w_in[:, o_r:o_g]),
        w_in[:, o_u:], jnp.pad(w_in[:, o_g:o_u], ((0, 0), (0, LANE - GLA_GATE_RANK)))], axis=1)
    wg = jnp.pad(w_g, ((0, LANE - GLA_GATE_RANK), (0, 0)))
    go = jnp.tile(jnp.pad(g_o, (0, pad)), GLA_HEADS)[None]
    wpool = jnp.zeros((POOL_DIM, POOL_DIM), F32)
    for gi in range(len(POOL_WINDOWS)):
        lo = gi * POOL_GDIM
        wpool = wpool.at[lo:lo + POOL_GDIM, lo:lo + POOL_GDIM].set(w_pool[gi])
    wo_gla = jnp.pad(w_out[:nv].reshape(GLA_HEADS, GLA_DV, -1), ((0, 0), (0, pad), (0, 0))).reshape(
        GLA_HEADS * GLA_DVP, -1)
    return dict(win=win.astype(BF16), wg=wg.astype(BF16), bg=b_g[None], go=go, wpool=wpool.astype(BF16),
                pscale=pool_scale[None], wo_gla=wo_gla.astype(BF16), wo_pool=w_out[nv:].astype(BF16))


def _odd_proj_body(x_ref, gmix_ref, win_ref, wg_ref, bg_ref, q_ref, k_ref, v_ref, r_ref, u_ref, g_ref):
    xn = _rms(x_ref[...], gmix_ref[...])
    z = _dot(xn, win_ref[...])
    nk = GLA_HEADS * GLA_DK
    nvp = GLA_HEADS * GLA_DVP
    q_ref[...] = z[:, :nk] * (GLA_DK ** -0.5)
    k_ref[...] = z[:, nk:2 * nk]
    v_ref[...] = z[:, 2 * nk:2 * nk + nvp]
    r_ref[...] = z[:, 2 * nk + nvp:2 * nk + 2 * nvp]
    o_u = 2 * nk + 2 * nvp
    u_ref[...] = z[:, o_u:o_u + POOL_DIM]
    zg = _dot(z[:, o_u + POOL_DIM:o_u + POOL_DIM + LANE], wg_ref[...]) + bg_ref[...]
    g_ref[...] = (jnp.minimum(zg, 0.0) - jnp.log1p(jnp.exp(-jnp.abs(zg)))) * (1.0 / GLA_TAU)


def _odd_proj(x, g_mix, ow, tm):
    m, d = x.shape
    nk = GLA_HEADS * GLA_DK
    nvp = GLA_HEADS * GLA_DVP
    row = lambda w: pl.BlockSpec((tm, w), lambda i: (i, 0))
    widths = [nk, nk, nvp, nvp, POOL_DIM, nk]
    return pl.pallas_call(
        _odd_proj_body, grid=(m // tm,),
        in_specs=[row(d), _const_spec((1, d)), _const_spec(ow["win"].shape), _const_spec(ow["wg"].shape),
                  _const_spec((1, nk))],
        out_specs=[row(w) for w in widths],
        out_shape=[jax.ShapeDtypeStruct((m, w), F32) for w in widths],
        compiler_params=_cparams("parallel"), name="odd_proj",
    )(x, g_mix[None], ow["win"], ow["wg"], ow["bg"])


def _gla_out(o, r, go):
    outs = []
    for h in range(GLA_HEADS):
        sl = slice(h * GLA_DVP, (h + 1) * GLA_DVP)
        oh = o[h] if isinstance(o, (list, tuple)) else o[:, sl]
        ms = jnp.sum(oh * oh, axis=-1, keepdims=True) * (1.0 / GLA_DV)
        rh = r[:, sl]
        outs.append(oh * lax.rsqrt(ms + EPS) * go[:, sl] * (rh / (1.0 + jnp.exp(-rh))))
    return jnp.concatenate(outs, axis=-1)


def _gla_scan_body(q_ref, k_ref, v_ref, g_ref, r_ref, go_ref, o_ref, st_ref, st_sc):
    c = pl.program_id(0)

    @pl.when(c == 0)
    def _():
        st_sc[...] = jnp.zeros(st_sc.shape, F32)

    nb, cs = q_ref.shape[:2]
    row = lax.broadcasted_iota(jnp.int32, (cs, cs), 0)
    col = lax.broadcasted_iota(jnp.int32, (cs, cs), 1)
    tril = (col <= row)
    tril_bf = tril.astype(BF16)
    for b in range(nb):
        outs = []
        for h in range(GLA_HEADS):
            ks = slice(h * GLA_DK, (h + 1) * GLA_DK)
            vs = slice(h * GLA_DVP, (h + 1) * GLA_DVP)
            q, k, v, g = q_ref[b, :, ks], k_ref[b, :, ks], v_ref[b, :, vs], g_ref[b, :, ks]
            g_hi = g.astype(BF16)
            g_r = g - g_hi.astype(F32)
            g_mid = g_r.astype(BF16)
            g_lo = (g_r - g_mid.astype(F32)).astype(BF16)
            gc = (jnp.dot(tril_bf, g_hi, preferred_element_type=F32)
                  + jnp.dot(tril_bf, g_mid, preferred_element_type=F32)
                  + jnp.dot(tril_bf, g_lo, preferred_element_type=F32))
            g_last = gc[cs - 1:cs, :]
            qd = q * jnp.exp(gc)
            a = jnp.where(tril, _dot_nt(qd, k * jnp.exp(-gc)), 0.0)
            st = st_sc[b, h]
            outs.append(_dot_nt(qd, st) + _dot(a, v))
            st_sc[b, h] = st * jnp.exp(g_last) + _dot_tn(v, k * jnp.exp(g_last - gc))
        o_ref[b] = _gla_out(outs, r_ref[b], go_ref[...])

    @pl.when(c == pl.num_programs(0) - 1)
    def _():
        st_ref[...] = st_sc[...]


def _gla_scan(q, k, v, g, r, go):
    n, s, nk = q.shape
    nvp = v.shape[2]
    cs = GLA_CHUNK
    blk = lambda w: pl.BlockSpec((n, cs, w), lambda c: (0, c, 0))
    st_shape = (n, GLA_HEADS, GLA_DVP, GLA_DK)
    return pl.pallas_call(
        _gla_scan_body, grid=(s // cs,),
        in_specs=[blk(nk), blk(nk), blk(nvp), blk(nk), blk(nvp), _const_spec((1, nvp))],
        out_specs=[blk(nvp), _const_spec(st_shape)],
        out_shape=[jax.ShapeDtypeStruct((n, s, nvp), F32), jax.ShapeDtypeStruct(st_shape, F32)],
        scratch_shapes=[pltpu.VMEM(st_shape, F32)],
        compiler_params=_cparams("arbitrary"), name="gla_scan",
    )(q, k, v, g, r, go)


def _pool_group_select(vals):
    grp = lax.broadcasted_iota(jnp.int32, (1, POOL_DIM), 1) // POOL_GDIM
    out = vals[-1]
    for gi in range(len(vals) - 2, -1, -1):
        out = jnp.where(grp == gi, vals[gi], out)
    return out


def _pool_body(u_ref, wp_ref, ps_ref, o_ref, ext_sc, *, tm):
    i = pl.program_id(1)
    hist = POOL_HIST + 1

    @pl.when(i == 0)
    def _():
        ext_sc[0:hist, :] = jnp.zeros((hist, POOL_DIM), F32)

    u = u_ref[0]
    ext_sc[hist:hist + tm, :] = u
    acc = u
    sums = []
    for k in range(1, max(POOL_WINDOWS)):
        acc = acc + ext_sc[hist - k:hist - k + tm, :]
        if k + 1 in POOL_WINDOWS:
            sums.append(acc)
    pos = i * tm + lax.broadcasted_iota(jnp.int32, (tm, 1), 0)
    win = _pool_group_select([jnp.full((1, POOL_DIM), w, jnp.int32) for w in POOL_WINDOWS])
    cnt = jnp.minimum(pos + 1, win).astype(F32)
    diff = _pool_group_select(sums) / cnt - u
    o_ref[0] = _dot(diff, wp_ref[...]) * ps_ref[...]
    ext_sc[0:hist, :] = ext_sc[tm:tm + hist, :]


def _pool_prompt(u, wpool, pscale, tm=512):
    n, s, pd = u.shape
    blk = pl.BlockSpec((1, tm, pd), lambda b, i: (b, i, 0))
    return pl.pallas_call(
        functools.partial(_pool_body, tm=tm), grid=(n, s // tm),
        in_specs=[blk, _const_spec((pd, pd)), _const_spec((1, pd))],
        out_specs=blk, out_shape=jax.ShapeDtypeStruct((n, s, pd), F32),
        scratch_shapes=[pltpu.VMEM((tm + POOL_HIST + 1, pd), F32)],
        compiler_params=_cparams("parallel", "arbitrary"), name="pool_prompt",
    )(u, wpool, pscale)


def _pool_sample_body(ue_ref, wp_ref, ps_ref, o_ref, *, past):
    ue = ue_ref[...]
    t = ue.shape[1]
    rowi = lax.broadcasted_iota(jnp.int32, (1, t, 1), 1)
    sums = [jnp.sum(jnp.where(rowi >= t - w, ue, 0.0), axis=1) for w in POOL_WINDOWS]
    cnt = _pool_group_select(
        [jnp.full((1, POOL_DIM), float(min(past + 1, w)), F32) for w in POOL_WINDOWS])
    diff = _pool_group_select(sums) / cnt - ue[:, t - 1, :]
    o_ref[...] = _dot(diff, wp_ref[...]) * ps_ref[...]


def _pool_sample(u_ext, wpool, pscale, past):
    b, t, pd = u_ext.shape
    return pl.pallas_call(
        functools.partial(_pool_sample_body, past=past), grid=(1,),
        in_specs=[_const_spec((b, t, pd)), _const_spec((pd, pd)), _const_spec((1, pd))],
        out_specs=_const_spec((b, pd)), out_shape=jax.ShapeDtypeStruct((b, pd), F32),
        compiler_params=_cparams("arbitrary"), name="pool_sample",
    )(u_ext, wpool, pscale)


def _gla_sample_body(q_ref, k_ref, v_ref, g_ref, r_ref, go_ref, st_ref, o_ref, sn_ref, o_sc, *, bs):
    eye = (lax.broadcasted_iota(jnp.int32, (GLA_DK, GLA_DK), 0)
           == lax.broadcasted_iota(jnp.int32, (GLA_DK, GLA_DK), 1))

    def col(rowv):
        return jnp.sum(jnp.where(eye, rowv, 0.0), axis=1, keepdims=True)

    o_sc[...] = jnp.zeros(o_sc.shape, F32)
    for b in range(bs):
        for h in range(GLA_HEADS):
            ks = slice(h * GLA_DK, (h + 1) * GLA_DK)
            q = q_ref[b:b + 1, ks]
            k = k_ref[b:b + 1, ks]
            eg = jnp.exp(g_ref[b:b + 1, ks])
            v = v_ref[b:b + 1, h * GLA_DVP:h * GLA_DVP + GLA_DV]
            s_new = st_ref[b, h] * col(eg) + col(k) * v
            sn_ref[b, h] = s_new
            o_sc[b:b + 1, h * GLA_DVP:h * GLA_DVP + GLA_DV] = jnp.sum(col(q) * s_new, axis=0, keepdims=True)
    o_ref[...] = _gla_out(o_sc[...], r_ref[...], go_ref[...])


def _gla_sample(q, k, v, g, r, go, state, bs=8):
    b, nk = q.shape
    nvp = v.shape[1]
    row = lambda w: pl.BlockSpec((bs, w), lambda i: (i, 0))
    st = pl.BlockSpec((bs, GLA_HEADS, GLA_DK, GLA_DV), lambda i: (i, 0, 0, 0))
    return pl.pallas_call(
        functools.partial(_gla_sample_body, bs=bs), grid=(b // bs,),
        in_specs=[row(nk), row(nk), row(nvp), row(nk), row(nvp), _const_spec((1, nvp)), st],
        out_specs=[row(nvp), st],
        out_shape=[jax.ShapeDtypeStruct((b, nvp), F32), jax.ShapeDtypeStruct(state.shape, F32)],
        scratch_shapes=[pltpu.VMEM((bs, nvp), F32)],
        compiler_params=_cparams("parallel"), name="gla_sample",
    )(q, k, v, g, r, go, state)


def _sample_q_body(qa_ref, gak_ref, wk_ref, selr_ref, selsw_ref, qabs_ref, qr_ref, qsw_ref):
    qa = qa_ref[...]
    lane = lax.broadcasted_iota(jnp.int32, (1, qa.shape[1]), 1)
    qg = qa.astype(F32) * gak_ref[...]
    for h in range(MLA_HEADS):
        base = MLA_PAIR * (h // 2) + MLA_NOPE * (h % 2)
        nope = (lane >= base) & (lane < base + MLA_NOPE)
        qabs_ref[h] = _dot_nt(jnp.where(nope, qg, 0.0), wk_ref[...]).astype(BF16)
        qr_ref[h] = jnp.dot(qa, selr_ref[h], preferred_element_type=F32).astype(BF16)
        qsw_ref[h] = jnp.dot(qa, selsw_ref[h], preferred_element_type=F32).astype(BF16)


def _sample_q(qa, gak, wk):
    b, width = qa.shape
    cols = _mla_cols()
    selr = np.zeros((MLA_HEADS, width, MLA_ROPE), np.float32)
    selsw = np.zeros((MLA_HEADS, width, MLA_ROPE), np.float32)
    half = MLA_ROPE // 2
    for h in range(MLA_HEADS):
        rc = cols[h, MLA_NOPE:]
        selr[h, rc, np.arange(MLA_ROPE)] = 1.0
        selsw[h, rc[half:], np.arange(half)] = 1.0
        selsw[h, rc[:half], half + np.arange(half)] = -1.0
    outs = pl.pallas_call(
        _sample_q_body, grid=(1,),
        in_specs=[_const_spec(qa.shape), _const_spec(gak.shape), _const_spec(wk.shape),
                  _const_spec(selr.shape), _const_spec(selsw.shape)],
        out_specs=[_const_spec((MLA_HEADS, b, MLA_KV_LORA)), _const_spec((MLA_HEADS, b, MLA_ROPE)),
                   _const_spec((MLA_HEADS, b, MLA_ROPE))],
        out_shape=[jax.ShapeDtypeStruct((MLA_HEADS, b, MLA_KV_LORA), BF16),
                   jax.ShapeDtypeStruct((MLA_HEADS, b, MLA_ROPE), BF16),
                   jax.ShapeDtypeStruct((MLA_HEADS, b, MLA_ROPE), BF16)],
        compiler_params=_cparams("arbitrary"), name="sample_q",
    )(qa, gak, wk, jnp.asarray(selr, BF16), jnp.asarray(selsw, BF16))
    return [jnp.swapaxes(o, 0, 1) for o in outs]


def _paged_body(pt_ref, *refs, npg):
    del pt_ref
    pages = refs[:4 * npg]
    (qabs_ref, qr_ref, qsw_ref, qb_ref, ckvn_ref, krn_ref, tc_ref, ts_ref, tcn_ref, tsn_ref,
     wuk_ref, e64_ref) = refs[4 * npg:4 * npg + 12]
    olat_ref, part_ref = refs[4 * npg + 12:4 * npg + 14]
    m_sc, l_sc, acc_sc, qcol_sc = refs[4 * npg + 14:]
    c = pl.program_id(1)
    rows = MOBA_HEADS
    eye = (lax.broadcasted_iota(jnp.int32, (MOBA_HD, LANE), 0)
           == lax.broadcasted_iota(jnp.int32, (MOBA_HD, LANE), 1))

    @pl.when(c == 0)
    def _():
        m_sc[...] = jnp.full(m_sc.shape, NEG_INF, F32)
        l_sc[...] = jnp.zeros(l_sc.shape, F32)
        acc_sc[...] = jnp.zeros(acc_sc.shape, F32)
        for h in range(MOBA_HEADS):
            qcol_sc[h] = jnp.broadcast_to(
                jnp.sum(jnp.where(eye, qb_ref[0, h:h + 1, :], 0.0), axis=1, keepdims=True), (MOBA_HD, LANE))

    qr, qsw = qr_ref[0], qsw_ref[0]

    def mla_update(ckv_b, kr_t, tc_t, ts_t, n_valid):
        kn = jnp.dot(ckv_b, wuk_ref[...], preferred_element_type=F32)
        mix = (jnp.dot(ckv_b, qabs_ref[0], preferred_element_type=F32)
               + jnp.dot((kn * kn).astype(BF16), e64_ref[...], preferred_element_type=F32))
        mix_t = mix.T
        ss = mix_t[rows:2 * rows] + jnp.sum(kr_t * kr_t, axis=0, keepdims=True)
        s = mix_t[0:rows] + _dot(qr, kr_t * tc_t) + _dot(qsw, kr_t * ts_t)
        s = s * lax.rsqrt(ss * (1.0 / MLA_QK) + EPS)
        if n_valid is not None:
            s = jnp.where(lax.broadcasted_iota(jnp.int32, s.shape, 1) < n_valid, s, NEG_INF)
        m_prev = m_sc[...]
        m_new = jnp.maximum(m_prev, jnp.max(s, axis=-1, keepdims=True))
        alpha = jnp.exp(m_prev - m_new)
        p = jnp.exp(s - m_new)
        l_sc[...] = alpha * l_sc[...] + jnp.sum(p, axis=-1, keepdims=True)
        acc_sc[...] = alpha * acc_sc[...] + jnp.dot(p.astype(BF16), ckv_b, preferred_element_type=F32)
        m_sc[...] = m_new

    mla_update(jnp.concatenate([pages[4 * j][...].astype(BF16) for j in range(npg)], axis=0),
               jnp.concatenate([pages[4 * j + 1][...] for j in range(npg)], axis=1),
               tc_ref[...], ts_ref[...], None)

    rowi = lax.broadcasted_iota(jnp.int32, (rows, LANE), 0)
    lanei = lax.broadcasted_iota(jnp.int32, (rows, LANE), 1)
    ppb = MOBA_BLOCK // PAGE_SIZE
    for g in range(npg // ppb):
        pg = [g * ppb + t for t in range(ppb)]
        raw = []
        for j in pg:
            s8 = jnp.zeros((rows, LANE), F32)
            for h in range(MOBA_HEADS):
                sh = jnp.sum(pages[4 * j + 2][h] * qcol_sc[h], axis=0, keepdims=True)
                s8 = jnp.where(rowi == h, sh, s8)
            raw.append(s8)
        gate = sum(jnp.sum(r, axis=-1, keepdims=True) for r in raw) * (1.0 / MOBA_BLOCK)
        sc = [r * (MOBA_HD ** -0.5) for r in raw]
        m = functools.reduce(jnp.maximum, [jnp.max(t, axis=-1, keepdims=True) for t in sc])
        p = [jnp.exp(t - m) for t in sc]
        l = sum(jnp.sum(t, axis=-1, keepdims=True) for t in p)
        o8 = jnp.zeros((rows, LANE), F32)
        for h in range(MOBA_HEADS):
            pv = sum(p[t][h:h + 1, :] * pages[4 * j + 3][h] for t, j in enumerate(pg))
            ocol = jnp.sum(pv, axis=1, keepdims=True)
            o8 = jnp.where(rowi == h, jnp.sum(jnp.where(eye, ocol, 0.0), axis=0, keepdims=True), o8)
        part_ref[0, g] = jnp.where(lanei == MOBA_HD, m,
                                   jnp.where(lanei == MOBA_HD + 1, l,
                                             jnp.where(lanei == MOBA_HD + 2, gate, o8)))

    @pl.when(c == pl.num_programs(1) - 1)
    def _():
        mla_update(ckvn_ref[0].astype(BF16), krn_ref[0], tcn_ref[...], tsn_ref[...], 1)
        olat_ref[0] = acc_sc[...] / l_sc[...]


def _paged_attend(page_table, layer, c_ckv, c_kr, c_k, c_v, qabs, qr, qsw, qb, ckv_new, kr_new, gk_rope,
                  wuk, npg=16):
    b, n_pages = page_table.shape
    past = n_pages * PAGE_SIZE
    rows = MOBA_HEADS
    nstep = n_pages // npg
    ppb = MOBA_BLOCK // PAGE_SIZE
    c, s = _rope_tables(jnp.arange(past + 1), MLA_ROPE // 2, MLA_THETA)
    tc = (jnp.concatenate([c, c], -1) * gk_rope[None]).T
    ts = (jnp.concatenate([s, s], -1) * gk_rope[None]).T
    e64 = np.zeros((MLA_HEADS * MLA_NOPE, LANE), np.float32)
    for h in range(MLA_HEADS):
        e64[h * MLA_NOPE:(h + 1) * MLA_NOPE, rows + h] = 1.0
    qabs_t = jnp.pad(jnp.swapaxes(qabs, 1, 2), ((0, 0), (0, 0), (0, LANE - rows)))

    def page_spec(shape, j):
        nd = len(shape)
        return pl.BlockSpec((None, None) + shape,
                            lambda i, cc, pt, j=j: (layer, pt[i, cc * npg + j]) + (0,) * nd)

    page_specs, page_args = [], []
    for j in range(npg):
        page_specs += [page_spec((PAGE_SIZE, MLA_KV_LORA), j), page_spec((MLA_ROPE, PAGE_SIZE), j),
                       page_spec((MOBA_HEADS, MOBA_HD, PAGE_SIZE), j),
                       page_spec((MOBA_HEADS, MOBA_HD, PAGE_SIZE), j)]
        page_args += [c_ckv, c_kr, c_k, c_v]
    per_sample = lambda r, w: pl.BlockSpec((1, r, w), lambda i, cc, pt: (i, 0, 0))
    const = lambda shape: pl.BlockSpec(shape, lambda i, cc, pt: (0,) * len(shape))
    tab = pl.BlockSpec((MLA_ROPE, npg * PAGE_SIZE), lambda i, cc, pt: (0, cc))
    in_specs = page_specs + [
        per_sample(MLA_KV_LORA, LANE), per_sample(rows, MLA_ROPE), per_sample(rows, MLA_ROPE),
        per_sample(rows, LANE), per_sample(PAGE_SIZE, MLA_KV_LORA), per_sample(MLA_ROPE, PAGE_SIZE),
        tab, tab, const((MLA_ROPE, PAGE_SIZE)), const((MLA_ROPE, PAGE_SIZE)), const(wuk.shape),
        const(e64.shape)]
    grid_spec = pltpu.PrefetchScalarGridSpec(
        num_scalar_prefetch=1, grid=(b, nstep), in_specs=in_specs,
        out_specs=[per_sample(rows, MLA_KV_LORA),
                   pl.BlockSpec((1, npg // ppb, rows, LANE), lambda i, cc, pt: (i, cc, 0, 0))],
        scratch_shapes=[pltpu.VMEM((rows, 1), F32), pltpu.VMEM((rows, 1), F32),
                        pltpu.VMEM((rows, MLA_KV_LORA), F32),
                        pltpu.VMEM((MOBA_HEADS, MOBA_HD, LANE), F32)])
    pad_rows = lambda t: jnp.pad(t[:, None, :], ((0, 0), (0, PAGE_SIZE - 1), (0, 0)))
    pad_cols = lambda t: jnp.pad(t[:, :, None], ((0, 0), (0, 0), (0, PAGE_SIZE - 1)))
    tnew = lambda t: jnp.broadcast_to(t[:, past:past + 1], (MLA_ROPE, PAGE_SIZE))
    qb_pad = jnp.pad(qb.reshape(b, MOBA_HEADS, MOBA_HD), ((0, 0), (0, 0), (0, LANE - MOBA_HD)))
    return pl.pallas_call(
        functools.partial(_paged_body, npg=npg), grid_spec=grid_spec,
        out_shape=[jax.ShapeDtypeStruct((b, rows, MLA_KV_LORA), F32),
                   jax.ShapeDtypeStruct((b, n_pages // ppb, rows, LANE), F32)],
        compiler_params=_cparams("parallel", "arbitrary"), name="paged_attend",
    )(page_table, *page_args, qabs_t, qr, qsw, qb_pad, pad_rows(ckv_new), pad_cols(kr_new),
      tc[:, :past], ts[:, :past], tnew(tc), tnew(ts), wuk, jnp.asarray(e64, BF16))


def _moba_combine_body(part_ref, q_ref, kn_ref, vn_ref, out_ref):
    part = part_ref[0]
    o = part[:, :, :MOBA_HD]
    m = part[:, :, MOBA_HD:MOBA_HD + 1]
    l = part[:, :, MOBA_HD + 1:MOBA_HD + 2]
    gate = part[:, :, MOBA_HD + 2:MOBA_HD + 3]
    sel = _top_blocks(gate, jnp.full(gate.shape, True), 0)
    s_own = jnp.sum(q_ref[0] * kn_ref[0], axis=-1, keepdims=True) * (MOBA_HD ** -0.5)
    m_sel = jnp.where(sel, m, NEG_INF)
    mt = jnp.maximum(jnp.max(m_sel, axis=0), s_own)
    w = jnp.exp(m_sel - mt)
    w_own = jnp.exp(s_own - mt)
    den = jnp.sum(w * l, axis=0) + w_own
    num = jnp.sum(w * o, axis=0) + w_own * vn_ref[0]
    out_ref[0] = num / den


def _moba_combine(part, qb, kb_new, vb_new):
    b, nb = part.shape[:2]
    hb = MOBA_HEADS * MOBA_HD
    r3 = lambda t: t.reshape(b, MOBA_HEADS, MOBA_HD)
    blk = pl.BlockSpec((1, MOBA_HEADS, MOBA_HD), lambda i: (i, 0, 0))
    out = pl.pallas_call(
        _moba_combine_body, grid=(b,),
        in_specs=[pl.BlockSpec((1, nb, MOBA_HEADS, LANE), lambda i: (i, 0, 0, 0)), blk, blk, blk],
        out_specs=blk, out_shape=jax.ShapeDtypeStruct((b, MOBA_HEADS, MOBA_HD), F32),
        compiler_params=_cparams("parallel"), name="moba_combine",
    )(part, r3(qb), r3(kb_new), r3(vb_new))
    return out.reshape(b, hb)


def _latent_out_body(olat_ref, wv_ref, o_ref):
    lane = lax.broadcasted_iota(jnp.int32, (1, MLA_HEADS * MLA_V), 1)
    wv = wv_ref[...]
    acc = jnp.zeros(o_ref.shape, F32)
    for h in range(MLA_HEADS):
        wh = jnp.where((lane >= h * MLA_V) & (lane < (h + 1) * MLA_V), wv, jnp.zeros_like(wv))
        acc = acc + _dot(olat_ref[h], wh)
    o_ref[...] = acc


def _latent_out(olat_hm, wv):
    _, b, _ = olat_hm.shape
    return pl.pallas_call(
        _latent_out_body, grid=(1,),
        in_specs=[_const_spec(olat_hm.shape), _const_spec(wv.shape)],
        out_specs=_const_spec((b, MLA_HEADS * MLA_V)),
        out_shape=jax.ShapeDtypeStruct((b, MLA_HEADS * MLA_V), F32),
        compiler_params=_cparams("arbitrary"), name="latent_out",
    )(olat_hm, wv)


def _even_layer(yp, ys, page_table, layer, c_ckv, c_kr, c_k, c_v, g_mix, w_in, g_cq, g_ckv, w_uq, w_uk, w_uv,
                g_aq, g_ak, g_bq, g_bk, w_out):
    n, s, d = yp.shape
    b = ys.shape[0]
    past = page_table.shape[1] * PAGE_SIZE
    hb = MOBA_HEADS * MOBA_HD
    ew = _even_weights(w_in, g_cq, g_ckv, w_uq, w_uk, w_uv, g_aq, g_ak, g_bq, g_bk)
    wo_a = w_out[:MLA_HEADS * MLA_V].astype(BF16)
    wo_b = w_out[MLA_HEADS * MLA_V:].astype(BF16)

    ckv, kr, qa, ka, va, qb, kb, vb, km = _even_proj(yp.reshape(n * s, d), g_mix, ew, jnp.arange(s),
                                                     MOBA_BLOCK)
    r3 = lambda t: t.reshape(n, s, t.shape[-1])
    oa = _mla_attn(r3(qa), r3(ka), r3(va))
    ob = _moba_attn(r3(qb), r3(kb), r3(vb), km.reshape(n, s // MOBA_BLOCK, hb))
    yp_new = _mm2_res(oa.reshape(n * s, -1), ob.reshape(n * s, -1), wo_a, wo_b, yp.reshape(n * s, d),
                      512).reshape(n, s, d)
    p_caches = (r3(ckv), r3(kr), kb.reshape(n, s, MOBA_HEADS, MOBA_HD), vb.reshape(n, s, MOBA_HEADS, MOBA_HD))

    sckv, skr, sqa, _, _, sqb, skb, svb, _ = _even_proj(
        ys.reshape(b, d), g_mix, ew, jnp.full((b,), past, jnp.int32), b)
    qabs, qr, qsw = _sample_q(sqa, ew["gak"], ew["wk"])
    wuk = w_uk.reshape(MLA_KV_LORA, -1).astype(BF16)
    olat, part = _paged_attend(page_table, layer, c_ckv, c_kr, c_k, c_v, qabs, qr, qsw, sqb, sckv, skr,
                               g_ak[MLA_NOPE:], wuk)
    soa = _latent_out(jnp.swapaxes(olat, 0, 1), ew["wv"])
    sob = _moba_combine(part, sqb, skb, svb)
    ys_new = _mm2_res(soa, sob, wo_a, wo_b, ys.reshape(b, d), b).reshape(b, 1, d)
    s_caches = (sckv[:, None], skr[:, None], skb.reshape(b, 1, MOBA_HEADS, MOBA_HD),
                svb.reshape(b, 1, MOBA_HEADS, MOBA_HD))
    return yp_new, ys_new, p_caches, s_caches


def _odd_layer(yp, ys, past, s_prev, u_prev, g_mix, w_in, w_g, b_g, g_o, w_pool, pool_scale, w_out):
    n, s, d = yp.shape
    b = ys.shape[0]
    ow = _odd_weights(w_in, w_g, b_g, g_o, w_pool, pool_scale, w_out)

    q, k, v, r, u, g = _odd_proj(yp.reshape(n * s, d), g_mix, ow, 512)
    r3 = lambda t: t.reshape(n, s, t.shape[-1])
    og, st_t = _gla_scan(r3(q), r3(k), r3(v), r3(g), r3(r), ow["go"])
    u3 = r3(u)
    pm = _pool_prompt(u3, ow["wpool"], ow["pscale"])
    yp_new = _mm2_res(og.reshape(n * s, -1), pm.reshape(n * s, -1), ow["wo_gla"], ow["wo_pool"],
                      yp.reshape(n * s, d), 512).reshape(n, s, d)
    p_state = jnp.swapaxes(st_t, 2, 3)[..., :GLA_DV]
    p_pool = u3[:, s - POOL_HIST:]

    sq, sk, sv, sr, su, sg = _odd_proj(ys.reshape(b, d), g_mix, ow, b)
    sog, s_new = _gla_sample(sq, sk, sv, sg, sr, ow["go"], s_prev)
    u_ext = jnp.concatenate([u_prev, su[:, None]], axis=1)
    spm = _pool_sample(u_ext, ow["wpool"], ow["pscale"], past)
    ys_new = _mm2_res(sog, spm, ow["wo_gla"], ow["wo_pool"], ys.reshape(b, d), b).reshape(b, 1, d)
    return yp_new, ys_new, (p_state, p_pool), (s_new, u_ext[:, 1:])


def kernel(x_prompt, x_sample, cache_mla_ckv, cache_mla_krope, cache_moba_k, cache_moba_v, state_gla, state_pool, cache_mem_k, cache_mem_v, page_table, mem_prompt, g_mix, g_memx, g_memm, w_mq, w_mk, w_mv, g_mq, g_mk, w_mo, g_ffn, w_ff1, w_ff2, ev_w_in, ev_g_cq, ev_g_ckv, ev_w_uq, ev_w_uk, ev_w_uv, ev_g_mla_q, ev_g_mla_k, ev_g_moba_q, ev_g_moba_k, ev_w_out, od_w_in, od_w_gate, od_b_gate, od_g_gla, od_w_pool, od_pool_scale, od_w_out):
    depth = g_mix.shape[0]
    n, s, d = x_prompt.shape
    b = x_sample.shape[0]
    past = page_table.shape[1] * PAGE_SIZE
    yp, ys = x_prompt, x_sample
    outs = {name: [] for name in ("p_ckv", "p_kr", "p_k", "p_v", "s_ckv", "s_kr", "s_k", "s_v",
                                  "p_gla", "p_pool", "s_gla", "s_pool", "p_mk", "p_mv")}
    for l in range(depth):
        i = l // 2
        if l % 2 == 0:
            yp, ys, pc, sc = _even_layer(
                yp, ys, page_table, i, cache_mla_ckv, jnp.transpose(cache_mla_krope, (0, 1, 3, 2)),
                jnp.transpose(cache_moba_k, (0, 1, 3, 4, 2)), jnp.transpose(cache_moba_v, (0, 1, 3, 4, 2)),
                g_mix[l], ev_w_in[i], ev_g_cq[i], ev_g_ckv[i], ev_w_uq[i], ev_w_uk[i], ev_w_uv[i],
                ev_g_mla_q[i], ev_g_mla_k[i], ev_g_moba_q[i], ev_g_moba_k[i], ev_w_out[i])
            for name, val in zip(("p_ckv", "p_kr", "p_k", "p_v"), pc):
                outs[name].append(val)
            for name, val in zip(("s_ckv", "s_kr", "s_k", "s_v"), sc):
                outs[name].append(val)
        else:
            yp, ys, ps, ss = _odd_layer(
                yp, ys, past, state_gla[i], state_pool[i], g_mix[l], od_w_in[i], od_w_gate[i], od_b_gate[i],
                od_g_gla[i], od_w_pool[i], od_pool_scale[i], od_w_out[i])
            for name, val in zip(("p_gla", "p_pool"), ps):
                outs[name].append(val)
            for name, val in zip(("s_gla", "s_pool"), ss):
                outs[name].append(val)
        mk, mv = _mem_kv(mem_prompt, g_memm[l], w_mk[l], w_mv[l], g_mk[l])
        ml = mk.shape[1]
        outs["p_mk"].append(mk.reshape(n, ml, MEM_HEADS, MEM_HD))
        outs["p_mv"].append(mv.reshape(n, ml, MEM_HEADS, MEM_HD))
        yp = _mem_attn(yp, g_memx[l], w_mq[l], g_mq[l], mk, mv, w_mo[l], 512)
        ys = _mem_attn_sample(ys.reshape(b, d), g_memx[l], w_mq[l], g_mq[l], cache_mem_k, cache_mem_v, l,
                              w_mo[l]).reshape(b, 1, d)
        w1 = w_ff1[l].astype(BF16)
        w2 = w_ff2[l].astype(BF16)
        yp = _ffn(yp.reshape(n * s, d), g_ffn[l], w1, w2, 1024).reshape(n, s, d)
        ys = _ffn(ys.reshape(b, d), g_ffn[l], w1, w2, b).reshape(b, 1, d)
    st = lambda name: jnp.stack(outs[name])
    return (yp, ys, st("p_ckv"), st("p_kr"), st("p_k"), st("p_v"), st("p_gla"), st("p_pool"),
            st("p_mk"), st("p_mv"), st("s_ckv"), st("s_kr"), st("s_k"), st("s_v"), st("s_gla"), st("s_pool"))
```

```python
import functools
import math

import jax
import jax.numpy as jnp
import numpy as np
from jax import lax
from jax.experimental import pallas as pl
from jax.experimental.pallas import tpu as pltpu

F32 = jnp.float32
BF16 = jnp.bfloat16
NEG_INF = float("-inf")

EPS = 1e-6
D_MODEL = 1024
PAGE_SIZE = 128
MLA_HEADS = 8
MLA_NOPE = 64
MLA_ROPE = 32
MLA_QK = MLA_NOPE + MLA_ROPE
MLA_V = 64
MLA_Q_LORA = 384
MLA_KV_LORA = 256
MLA_THETA = 10000.0
MLA_PAIR = 256
MOBA_HEADS = 8
MOBA_HD = 64
MOBA_ROT = MOBA_HD // 4
MOBA_BLOCK = 256
MOBA_TOPK = 3
ROPE_THETA = 500000.0
GLA_HEADS = 4
GLA_DK = 128
GLA_DV = 192
GLA_DVP = 256
GLA_GATE_RANK = 16
GLA_TAU = 16.0
GLA_CHUNK = 64
POOL_WINDOWS = (2, 4, 8, 16)
POOL_GDIM = 64
POOL_DIM = POOL_GDIM * len(POOL_WINDOWS)
POOL_HIST = max(POOL_WINDOWS) - 1
MEM_HEADS = 4
MEM_HD = 128
LANE = 128
VMEM_LIMIT = 56 * 1024 * 1024


def _rms(x, g):
    return x * lax.rsqrt(jnp.mean(x * x, axis=-1, keepdims=True) + EPS) * g


def _dot(a, b):
    return jnp.dot(a.astype(BF16), b.astype(BF16), preferred_element_type=F32)


def _dot_nt(a, b):
    return lax.dot_general(a.astype(BF16), b.astype(BF16), (((1,), (1,)), ((), ())),
                           preferred_element_type=F32)


def _dot_tn(a, b):
    return lax.dot_general(a.astype(BF16), b.astype(BF16), (((0,), (0,)), ((), ())),
                           preferred_element_type=F32)


def _split(a):
    hi = a.astype(BF16)
    lo = (a - hi.astype(F32)).astype(BF16)
    return hi, lo


def _dot_hl(a, sel):
    hi, lo = _split(a)
    return (jnp.dot(hi, sel, preferred_element_type=F32)
            + jnp.dot(lo, sel, preferred_element_type=F32))


def _cparams(*sem):
    return pltpu.CompilerParams(dimension_semantics=sem, vmem_limit_bytes=VMEM_LIMIT)


def _const_spec(shape):
    nd = len(shape)
    return pl.BlockSpec(shape, lambda *_: (0,) * nd)


def _mla_cols():
    cols = np.zeros((MLA_HEADS, MLA_QK), np.int32)
    for h in range(MLA_HEADS):
        base = MLA_PAIR * (h // 2)
        cols[h, :MLA_NOPE] = base + MLA_NOPE * (h % 2) + np.arange(MLA_NOPE)
        cols[h, MLA_NOPE:] = base + 2 * MLA_NOPE + MLA_ROPE * (h % 2) + np.arange(MLA_ROPE)
    return cols


def _rope_tables(pos, half, theta):
    inv = jnp.exp(jnp.arange(half, dtype=F32) * (-math.log(theta) / half))
    ang = pos.astype(F32)[:, None] * inv[None, :]
    return jnp.cos(ang), jnp.sin(ang)


def _mla_pair_tables(pos):
    c, s = _rope_tables(pos, MLA_ROPE // 2, MLA_THETA)
    p = pos.shape[0]
    one = jnp.ones((p, 2 * MLA_NOPE), F32)
    pad1 = jnp.ones((p, MLA_PAIR - 2 * MLA_NOPE - 2 * MLA_ROPE), F32)
    ct = jnp.concatenate([one, c, c, c, c, pad1], -1)
    st = jnp.concatenate([0 * one, s, s, s, s, 0 * pad1], -1)
    return ct, st


def _moba_pair_tables(pos):
    c, s = _rope_tables(pos, MOBA_ROT // 2, ROPE_THETA)
    p = pos.shape[0]
    one = jnp.ones((p, MOBA_HD - MOBA_ROT), F32)
    ch = jnp.concatenate([c, c, one], -1)
    sh = jnp.concatenate([s, s, 0 * one], -1)
    return jnp.concatenate([ch, ch], -1), jnp.concatenate([sh, sh], -1)


def _even_weights(w_in, g_cq, g_ckv, w_uq, w_uk, w_uv, g_aq, g_ak, g_bq, g_bk):
    i0 = MLA_Q_LORA + MLA_KV_LORA
    i1 = i0 + MLA_ROPE
    idx = np.concatenate([np.arange(0, i0), np.arange(i1, w_in.shape[1]), np.arange(i0, i1)])
    win = jnp.pad(w_in[:, idx], ((0, 0), (0, LANE - MLA_ROPE))).astype(BF16)
    cols = _mla_cols()
    width = MLA_PAIR * MLA_HEADS // 2
    wq = jnp.zeros((MLA_Q_LORA, width), F32).at[:, cols.reshape(-1)].set(
        w_uq.reshape(MLA_Q_LORA, -1)).astype(BF16)
    wk = jnp.zeros((MLA_KV_LORA, width), F32).at[:, cols[:, :MLA_NOPE].reshape(-1)].set(
        w_uk.reshape(MLA_KV_LORA, -1)).astype(BF16)
    wv = w_uv.reshape(MLA_KV_LORA, -1).astype(BF16)
    pk = np.zeros((MLA_ROPE, width), np.float32)
    e = np.zeros((width, LANE), np.float32)
    for h in range(MLA_HEADS):
        pk[np.arange(MLA_ROPE), cols[h, MLA_NOPE:]] = 1.0
        e[cols[h], h] = 1.0
    gaq = jnp.zeros((1, width), F32).at[0, cols.reshape(-1)].set(jnp.tile(g_aq, MLA_HEADS))
    gak = jnp.zeros((1, width), F32).at[0, cols.reshape(-1)].set(jnp.tile(g_ak, MLA_HEADS))
    b64 = np.kron(np.eye(MOBA_HEADS, dtype=np.float32), np.ones((MOBA_HD, MOBA_HD), np.float32))
    return dict(
        win=win, gcq=g_cq[None], gckv=g_ckv[None], wq=wq, wk=wk, wv=wv,
        pk=jnp.asarray(pk, BF16), e=jnp.asarray(e, BF16), et=jnp.asarray(e.T, BF16),
        gaq=gaq, gak=gak, b64=jnp.asarray(b64, BF16),
        gbq=jnp.tile(g_bq, MOBA_HEADS)[None], gbk=jnp.tile(g_bk, MOBA_HEADS)[None])


def _even_proj_body(x_ref, gmix_ref, win_ref, gcq_ref, gckv_ref, wq_ref, wk_ref, wv_ref, pk_ref,
                    e_ref, et_ref, gaq_ref, gak_ref, ca_ref, sa_ref, b64_ref, gbq_ref, gbk_ref,
                    cb_ref, sb_ref,
                    ckv_ref, kr_ref, qa_ref, ka_ref, va_ref, qb_ref, kb_ref, vb_ref, km_ref):
    xn = _rms(x_ref[...], gmix_ref[...])
    z = _dot(xn, win_ref[...])
    o_ckv = MLA_Q_LORA
    o_qb = o_ckv + MLA_KV_LORA
    hb = MOBA_HEADS * MOBA_HD
    cqn = _rms(z[:, :o_ckv], gcq_ref[...])
    ckv = _rms(z[:, o_ckv:o_qb], gckv_ref[...])
    qb = z[:, o_qb:o_qb + hb]
    kb = z[:, o_qb + hb:o_qb + 2 * hb]
    vb = z[:, o_qb + 2 * hb:o_qb + 3 * hb]
    kr = z[:, o_qb + 3 * hb:o_qb + 3 * hb + MLA_ROPE]
    ckv_ref[...] = ckv
    kr_ref[...] = kr
    vb_ref[...] = vb

    npair = MLA_HEADS // 2
    ca = jnp.concatenate([ca_ref[...]] * npair, axis=-1)
    sa = jnp.concatenate([sa_ref[...]] * npair, axis=-1)
    width = MLA_PAIR * npair
    lane = lax.broadcasted_iota(jnp.int32, (1, width), 1)
    in_pair = lane % MLA_PAIR
    first_half = ((in_pair >= 2 * MLA_NOPE) & (in_pair < 2 * MLA_NOPE + 2 * MLA_ROPE)
                  & (lane % MLA_ROPE < MLA_ROPE // 2))

    def norm_rope_a(t, g):
        ss = _dot_hl(t * t, e_ref[...])
        rs = lax.rsqrt(ss * (1.0 / MLA_QK) + EPS)
        tn = t * _dot_hl(rs, et_ref[...]) * g
        sw = jnp.where(first_half, -pltpu.roll(tn, width - MLA_ROPE // 2, 1),
                       pltpu.roll(tn, MLA_ROPE // 2, 1))
        return tn * ca + sw * sa

    qa = norm_rope_a(_dot(cqn, wq_ref[...]), gaq_ref[...]) * (MLA_QK ** -0.5)
    ka = norm_rope_a(_dot(ckv, wk_ref[...]) + _dot_hl(kr, pk_ref[...]), gak_ref[...])
    qa_ref[...] = qa.astype(BF16)
    ka_ref[...] = ka.astype(BF16)
    va_ref[...] = _dot(ckv, wv_ref[...]).astype(BF16)

    cb = jnp.concatenate([cb_ref[...]] * (MOBA_HEADS // 2), axis=-1)
    sb = jnp.concatenate([sb_ref[...]] * (MOBA_HEADS // 2), axis=-1)
    lane_b = lax.broadcasted_iota(jnp.int32, (1, hb), 1)
    first_half_b = lane_b % MOBA_HD < MOBA_ROT // 2

    def norm_rope_b(t, g):
        ss = _dot_hl(t * t, b64_ref[...])
        tn = t * lax.rsqrt(ss * (1.0 / MOBA_HD) + EPS) * g
        sw = jnp.where(first_half_b, -pltpu.roll(tn, hb - MOBA_ROT // 2, 1),
                       pltpu.roll(tn, MOBA_ROT // 2, 1))
        return tn * cb + sw * sb

    qb_ref[...] = norm_rope_b(qb, gbq_ref[...])
    kbn = norm_rope_b(kb, gbk_ref[...])
    kb_ref[...] = kbn
    km_ref[0] = jnp.mean(kbn, axis=0, keepdims=True)


def _even_proj(x, g_mix, ew, pos, tm):
    m, d = x.shape
    nt = m // tm
    ca, sa = _mla_pair_tables(pos)
    cb, sb = _moba_pair_tables(pos)
    npt = pos.shape[0] // tm
    row = lambda w: pl.BlockSpec((tm, w), lambda i: (i, 0))
    tab = lambda w: pl.BlockSpec((tm, w), lambda i: (i % npt, 0))
    hb = MOBA_HEADS * MOBA_HD
    wa = MLA_PAIR * MLA_HEADS // 2
    consts = [g_mix[None], ew["win"], ew["gcq"], ew["gckv"], ew["wq"], ew["wk"], ew["wv"], ew["pk"],
              ew["e"], ew["et"], ew["gaq"], ew["gak"]]
    consts2 = [ew["b64"], ew["gbq"], ew["gbk"]]
    in_specs = ([row(d)] + [_const_spec(c.shape) for c in consts] + [tab(MLA_PAIR), tab(MLA_PAIR)]
                + [_const_spec(c.shape) for c in consts2] + [tab(LANE), tab(LANE)])
    out_shape = [
        jax.ShapeDtypeStruct((m, MLA_KV_LORA), F32), jax.ShapeDtypeStruct((m, MLA_ROPE), F32),
        jax.ShapeDtypeStruct((m, wa), BF16), jax.ShapeDtypeStruct((m, wa), BF16),
        jax.ShapeDtypeStruct((m, MLA_HEADS * MLA_V), BF16),
        jax.ShapeDtypeStruct((m, hb), F32), jax.ShapeDtypeStruct((m, hb), F32),
        jax.ShapeDtypeStruct((m, hb), F32), jax.ShapeDtypeStruct((nt, 1, hb), F32)]
    out_specs = [row(MLA_KV_LORA), row(MLA_ROPE), row(wa), row(wa), row(MLA_HEADS * MLA_V),
                 row(hb), row(hb), row(hb), pl.BlockSpec((1, 1, hb), lambda i: (i, 0, 0))]
    return pl.pallas_call(
        _even_proj_body, grid=(nt,), in_specs=in_specs, out_specs=out_specs, out_shape=out_shape,
        compiler_params=_cparams("parallel"), name="even_proj",
    )(x, *consts, ca, sa, *consts2, cb, sb)


def _mla_head_mask(e):
    lane = lax.broadcasted_iota(jnp.int32, (1, MLA_PAIR), 1)
    nope = (lane >= MLA_NOPE * e) & (lane < MLA_NOPE * (e + 1))
    r0 = 2 * MLA_NOPE + MLA_ROPE * e
    return nope | ((lane >= r0) & (lane < r0 + MLA_ROPE))


def _mla_attn_body(q_ref, k_ref, v_ref, o_ref, s_sc, mrun_sc, lrun_sc, acc_sc, *, tq):
    qi = pl.program_id(2)
    q = q_ref[0]
    qs = [jnp.where(_mla_head_mask(e), q, jnp.zeros_like(q)) for e in (0, 1)]
    causal = (lax.broadcasted_iota(jnp.int32, (tq, tq), 1)
              <= lax.broadcasted_iota(jnp.int32, (tq, tq), 0))
    mrun_sc[...] = jnp.full(mrun_sc.shape, NEG_INF, F32)

    def scores(ki, diagonal):
        k = k_ref[0, pl.ds(pl.multiple_of(ki * tq, tq), tq), :]
        for e in (0, 1):
            s = _dot_nt(qs[e], k)
            if diagonal:
                s = jnp.where(causal, s, NEG_INF)
            s_sc[e, ki] = s
            mrun_sc[e] = jnp.maximum(mrun_sc[e], s)

    def score_step(ki, carry):
        scores(ki, False)
        return carry

    lax.fori_loop(0, qi, score_step, 0)
    scores(qi, True)
    for e in (0, 1):
        mrun_sc[e] = jnp.broadcast_to(jnp.max(mrun_sc[e], axis=-1, keepdims=True), (tq, tq))
    lrun_sc[...] = jnp.zeros(lrun_sc.shape, F32)
    acc_sc[...] = jnp.zeros(acc_sc.shape, F32)

    def value_step(ki, carry):
        v = v_ref[0, pl.ds(pl.multiple_of(ki * tq, tq), tq), :]
        for e in (0, 1):
            p = jnp.exp(s_sc[e, ki] - mrun_sc[e])
            lrun_sc[e] += p
            acc_sc[e] += _dot(p, v)
        return carry

    lax.fori_loop(0, qi + 1, value_step, 0)
    lane_o = lax.broadcasted_iota(jnp.int32, (1, 2 * MLA_V), 1)
    o0 = acc_sc[0] / jnp.sum(lrun_sc[0], axis=-1, keepdims=True)
    o1 = acc_sc[1] / jnp.sum(lrun_sc[1], axis=-1, keepdims=True)
    o_ref[0] = jnp.where(lane_o < MLA_V, o0, o1)


def _mla_attn(qa, ka, va, tq=512):
    n, s, _ = qa.shape
    npair = MLA_HEADS // 2
    return pl.pallas_call(
        functools.partial(_mla_attn_body, tq=tq),
        grid=(n, npair, s // tq),
        in_specs=[pl.BlockSpec((1, tq, MLA_PAIR), lambda b, p, i: (b, i, p)),
                  pl.BlockSpec((1, s, MLA_PAIR), lambda b, p, i: (b, 0, p)),
                  pl.BlockSpec((1, s, 2 * MLA_V), lambda b, p, i: (b, 0, p))],
        out_specs=pl.BlockSpec((1, tq, 2 * MLA_V), lambda b, p, i: (b, i, p)),
        out_shape=jax.ShapeDtypeStruct((n, s, MLA_HEADS * MLA_V), F32),
        scratch_shapes=[pltpu.VMEM((2, s // tq, tq, tq), F32), pltpu.VMEM((2, tq, tq), F32),
                        pltpu.VMEM((2, tq, tq), F32), pltpu.VMEM((2, tq, 2 * MLA_V), F32)],
        compiler_params=_cparams("parallel", "parallel", "arbitrary"), name="mla_attn",
    )(qa, ka, va)


def _top_blocks(gate, eligible, axis):
    idx = lax.broadcasted_iota(jnp.int32, gate.shape, axis)
    big = jnp.int32(1 << 20)
    g = jnp.where(eligible, gate, NEG_INF)
    sel = jnp.zeros(gate.shape, jnp.bool_)
    for _ in range(MOBA_TOPK):
        mx = jnp.max(g, axis=axis, keepdims=True)
        cand = (g == mx) & (mx > NEG_INF)
        first = jnp.min(jnp.where(cand, idx, big), axis=axis, keepdims=True)
        pick = idx == first
        sel = sel | pick
        g = jnp.where(pick, NEG_INF, g)
    return sel


def _moba_attn_body(q_ref, k_ref, vt_ref, km_ref, o_ref, s_sc, sel_sc, mrun_sc, lrun_sc, acc_sc, *, tq):
    qi = pl.program_id(2)
    q = q_ref[0]
    lane = lax.broadcasted_iota(jnp.int32, (1, 2 * MOBA_HD), 1)
    nblk = km_ref.shape[1]
    blk = lax.broadcasted_iota(jnp.int32, (nblk, tq), 0)
    km_hi, km_lo = _split(km_ref[0])
    qs = []
    for e in (0, 1):
        qe = jnp.where((lane >= MOBA_HD * e) & (lane < MOBA_HD * (e + 1)), q, 0.0)
        qs.append((qe * (MOBA_HD ** -0.5)).astype(BF16))
        q_hi, q_lo = _split(qe)
        nt = (((1,), (1,)), ((), ()))
        gate = (lax.dot_general(km_hi, q_hi, nt, preferred_element_type=F32)
                + lax.dot_general(km_lo, q_hi, nt, preferred_element_type=F32)
                + lax.dot_general(km_hi, q_lo, nt, preferred_element_type=F32))
        sel_sc[e] = _top_blocks(gate, blk < qi, 0).astype(F32)
    causal = (lax.broadcasted_iota(jnp.int32, (tq, tq), 0)
              <= lax.broadcasted_iota(jnp.int32, (tq, tq), 1))

    def scores(j, own):
        k = k_ref[0, pl.ds(pl.multiple_of(j * tq, tq), tq), :].astype(BF16)
        for e in (0, 1):
            ok = causal if own else sel_sc[e, pl.ds(j, 1), :] > 0.5
            s = jnp.where(ok, _dot_nt(k, qs[e]), NEG_INF)
            s_sc[e, j] = s
            if own:
                mrun_sc[e] = s
            else:
                mrun_sc[e] = jnp.maximum(mrun_sc[e], s)

    def score_step(j, carry):
        scores(j, False)
        return carry

    scores(qi, True)
    lax.fori_loop(0, qi, score_step, 0)
    for e in (0, 1):
        mrun_sc[e] = jnp.broadcast_to(jnp.max(mrun_sc[e], axis=0, keepdims=True), (tq, tq))
    lrun_sc[...] = jnp.zeros(lrun_sc.shape, F32)
    acc_sc[...] = jnp.zeros(acc_sc.shape, F32)

    def value_step(j, carry):
        vt = vt_ref[0, 0, j].astype(BF16)
        for e in (0, 1):
            p = jnp.exp(s_sc[e, j] - mrun_sc[e])
            lrun_sc[e] += p
            acc_sc[e] += jnp.dot(vt, p.astype(BF16), preferred_element_type=F32)
        return carry

    lax.fori_loop(0, qi + 1, value_step, 0)
    o0 = acc_sc[0] / jnp.sum(lrun_sc[0], axis=0, keepdims=True)
    o1 = acc_sc[1] / jnp.sum(lrun_sc[1], axis=0, keepdims=True)
    drow = lax.broadcasted_iota(jnp.int32, (2 * MOBA_HD, 1), 0)
    o_ref[0] = jnp.where(drow < MOBA_HD, o0, o1).T


def _moba_attn(qb, kb, vb, km):
    n, s, hb = qb.shape
    tq = MOBA_BLOCK
    nblk = s // tq
    npair = hb // LANE
    vt = vb.reshape(n, nblk, tq, npair, LANE).transpose(0, 3, 1, 4, 2)
    return pl.pallas_call(
        functools.partial(_moba_attn_body, tq=tq),
        grid=(n, npair, nblk),
        in_specs=[pl.BlockSpec((1, tq, LANE), lambda b, p, i: (b, i, p)),
                  pl.BlockSpec((1, s, LANE), lambda b, p, i: (b, 0, p)),
                  pl.BlockSpec((1, 1, nblk, LANE, tq), lambda b, p, i: (b, p, 0, 0, 0)),
                  pl.BlockSpec((1, nblk, LANE), lambda b, p, i: (b, 0, p))],
        out_specs=pl.BlockSpec((1, tq, LANE), lambda b, p, i: (b, i, p)),
        out_shape=jax.ShapeDtypeStruct((n, s, hb), F32),
        scratch_shapes=[pltpu.VMEM((2, nblk, tq, tq), F32), pltpu.VMEM((2, nblk, tq), F32),
                        pltpu.VMEM((2, tq, tq), F32), pltpu.VMEM((2, tq, tq), F32),
                        pltpu.VMEM((2, LANE, tq), F32)],
        compiler_params=_cparams("parallel", "parallel", "arbitrary"), name="moba_attn",
    )(qb, kb, vt, km)


def _mm2_res_body(a_ref, b_ref, wa_ref, wb_ref, r_ref, o_ref):
    o_ref[...] = r_ref[...] + _dot(a_ref[...], wa_ref[...]) + _dot(b_ref[...], wb_ref[...])


def _mm2_res(a, b, wa, wb, res, tm):
    m, d = res.shape
    row = lambda w: pl.BlockSpec((tm, w), lambda i: (i, 0))
    return pl.pallas_call(
        _mm2_res_body, grid=(m // tm,),
        in_specs=[row(a.shape[1]), row(b.shape[1]), _const_spec(wa.shape), _const_spec(wb.shape), row(d)],
        out_specs=row(d), out_shape=jax.ShapeDtypeStruct((m, d), F32),
        compiler_params=_cparams("parallel"), name="mix_out",
    )(a, b, wa, wb, res)


def _mem_kv_body(mem_ref, gm_ref, wk_ref, wv_ref, gk_ref, k_ref, v_ref):
    mn = _rms(mem_ref[0], gm_ref[...])
    k = _dot(mn, wk_ref[...])
    for h in range(MEM_HEADS):
        sl = slice(h * MEM_HD, (h + 1) * MEM_HD)
        k_ref[0, :, sl] = _rms(k[:, sl], gk_ref[...])
    v_ref[0] = _dot(mn, wv_ref[...])


def _mem_kv(mem, g_m, w_k, w_v, g_k):
    n, ml, d = mem.shape
    hd = MEM_HEADS * MEM_HD
    blk = lambda w: pl.BlockSpec((1, ml, w), lambda b: (b, 0, 0))
    return pl.pallas_call(
        _mem_kv_body, grid=(n,),
        in_specs=[blk(d), _const_spec((1, d)), _const_spec((d, hd)), _const_spec((d, hd)),
                  _const_spec((1, MEM_HD))],
        out_specs=[blk(hd), blk(hd)],
        out_shape=[jax.ShapeDtypeStruct((n, ml, hd), F32)] * 2,
        compiler_params=_cparams("parallel"), name="mem_kv",
    )(mem, g_m[None], w_k.reshape(d, hd).astype(BF16), w_v.reshape(d, hd).astype(BF16), g_k[None])


def _mem_attn_body(x_ref, g_ref, wq_ref, gq_ref, k_ref, v_ref, wo_ref, o_ref):
    x = x_ref[0]
    q = _dot(_rms(x, g_ref[...]), wq_ref[...])
    k = k_ref[0].astype(BF16)
    v = v_ref[0].astype(BF16)
    outs = []
    for h in range(MEM_HEADS):
        sl = slice(h * MEM_HD, (h + 1) * MEM_HD)
        qh = _rms(q[:, sl], gq_ref[...])
        s = _dot_nt(qh, k[:, sl]) * (MEM_HD ** -0.5)
        p = jnp.exp(s - jnp.max(s, axis=-1, keepdims=True))
        p = p / jnp.sum(p, axis=-1, keepdims=True)
        outs.append(_dot(p, v[:, sl]))
    o_ref[0] = x + _dot(jnp.concatenate(outs, axis=-1), wo_ref[...])


def _mem_attn(x, g, w_q, g_q, k, v, w_o, tm):
    n, s, d = x.shape
    ml, hd = k.shape[1:]
    return pl.pallas_call(
        _mem_attn_body, grid=(n, s // tm),
        in_specs=[pl.BlockSpec((1, tm, d), lambda b, i: (b, i, 0)), _const_spec((1, d)),
                  _const_spec((d, hd)), _const_spec((1, MEM_HD)),
                  pl.BlockSpec((1, ml, hd), lambda b, i: (b, 0, 0)),
                  pl.BlockSpec((1, ml, hd), lambda b, i: (b, 0, 0)), _const_spec((hd, d))],
        out_specs=pl.BlockSpec((1, tm, d), lambda b, i: (b, i, 0)),
        out_shape=jax.ShapeDtypeStruct((n, s, d), F32),
        compiler_params=_cparams("parallel", "parallel"), name="mem_attn",
    )(x, g[None], w_q.reshape(d, hd).astype(BF16), g_q[None], k, v, w_o.reshape(hd, d).astype(BF16))


def _mem_attn_sample_body(x_ref, g_ref, wq_ref, gq_ref, k_ref, v_ref, wo_ref, o_ref, q_sc, att_sc, *, bs):
    x = x_ref[...]
    q = _dot(_rms(x, g_ref[...]), wq_ref[...])
    for h in range(MEM_HEADS):
        q_sc[:, h, :] = _rms(q[:, h * MEM_HD:(h + 1) * MEM_HD], gq_ref[...]) * (MEM_HD ** -0.5)
    for b in range(bs):
        s = jnp.sum(k_ref[b] * q_sc[b], axis=-1, keepdims=True)
        p = jnp.exp(s - jnp.max(s, axis=0, keepdims=True))
        o = jnp.sum(p * v_ref[b], axis=0) / jnp.sum(p, axis=0)
        for h in range(MEM_HEADS):
            att_sc[b:b + 1, h * MEM_HD:(h + 1) * MEM_HD] = o[h:h + 1, :]
    o_ref[...] = x + _dot(att_sc[...], wo_ref[...])


def _mem_attn_sample(x, g, w_q, g_q, cache_k, cache_v, layer, w_o, bs=8):
    b, d = x.shape
    ml = cache_k.shape[2]
    hd = MEM_HEADS * MEM_HD
    mem = pl.BlockSpec((None, bs, ml, MEM_HEADS, MEM_HD), lambda i: (layer, i, 0, 0, 0))
    return pl.pallas_call(
        functools.partial(_mem_attn_sample_body, bs=bs), grid=(b // bs,),
        in_specs=[pl.BlockSpec((bs, d), lambda i: (i, 0)), _const_spec((1, d)),
                  _const_spec((d, hd)), _const_spec((1, MEM_HD)), mem, mem, _const_spec((hd, d))],
        out_specs=pl.BlockSpec((bs, d), lambda i: (i, 0)),
        out_shape=jax.ShapeDtypeStruct((b, d), F32),
        scratch_shapes=[pltpu.VMEM((bs, MEM_HEADS, MEM_HD), F32), pltpu.VMEM((bs, hd), F32)],
        compiler_params=_cparams("parallel"), name="mem_attn_sample",
    )(x, g[None], w_q.reshape(d, hd).astype(BF16), g_q[None], cache_k, cache_v,
      w_o.reshape(hd, d).astype(BF16))


def _ffn_body(x_ref, g_ref, w1_ref, w2_ref, o_ref, xn_sc, acc_sc):
    j = pl.program_id(1)

    @pl.when(j == 0)
    def _():
        xn_sc[...] = _rms(x_ref[...], g_ref[...]).astype(BF16)
        acc_sc[...] = jnp.zeros(acc_sc.shape, F32)

    h = jnp.maximum(jnp.dot(xn_sc[...], w1_ref[...], preferred_element_type=F32), 0.0)
    acc_sc[...] += _dot(h * h, w2_ref[...])

    @pl.when(j == pl.num_programs(1) - 1)
    def _():
        o_ref[...] = x_ref[...] + acc_sc[...]


def _ffn(x, g, w1, w2, tm, tf=512):
    m, d = x.shape
    ff = w1.shape[1]
    return pl.pallas_call(
        _ffn_body, grid=(m // tm, ff // tf),
        in_specs=[pl.BlockSpec((tm, d), lambda i, j: (i, 0)), _const_spec((1, d)),
                  pl.BlockSpec((d, tf), lambda i, j: (0, j)), pl.BlockSpec((tf, d), lambda i, j: (j, 0))],
        out_specs=pl.BlockSpec((tm, d), lambda i, j: (i, 0)),
        out_shape=jax.ShapeDtypeStruct((m, d), F32),
        scratch_shapes=[pltpu.VMEM((tm, d), BF16), pltpu.VMEM((tm, d), F32)],
        compiler_params=_cparams("parallel", "arbitrary"), name="ffn",
    )(x, g[None], w1, w2)


def _odd_weights(w_in, w_g, b_g, g_o, w_pool, pool_scale, w_out):
    nk = GLA_HEADS * GLA_DK
    nv = GLA_HEADS * GLA_DV
    pad = GLA_DVP - GLA_DV

    def pad_heads_cols(w):
        return jnp.pad(w.reshape(w.shape[0], GLA_HEADS, GLA_DV), ((0, 0), (0, 0), (0, pad))).reshape(
            w.shape[0], GLA_HEADS * GLA_DVP)

    o_v, o_r, o_g = 2 * nk, 2 * nk + nv, 2 * nk + 2 * nv
    o_u = o_g + GLA_GATE_RANK
    win = jnp.concatenate([
        w_in[:, :o_v], pad_heads_cols(w_in[:, o_v:o_r]), pad_heads_cols(w_in[:, o_r:o_g]),
        w_in[:, o_u:], jnp.pad(w_in[:, o_g:o_u], ((0, 0), (0, LANE - GLA_GATE_RANK)))], axis=1)
    wg = jnp.pad(w_g, ((0, LANE - GLA_GATE_RANK), (0, 0)))
    go = jnp.tile(jnp.pad(g_o, (0, pad)), GLA_HEADS)[None]
    wpool = jnp.zeros((POOL_DIM, POOL_DIM), F32)
    for gi in range(len(POOL_WINDOWS)):
        lo = gi * POOL_GDIM
        wpool = wpool.at[lo:lo + POOL_GDIM, lo:lo + POOL_GDIM].set(w_pool[gi])
    wo_gla = jnp.pad(w_out[:nv].reshape(GLA_HEADS, GLA_DV, -1), ((0, 0), (0, pad), (0, 0))).reshape(
        GLA_HEADS * GLA_DVP, -1)
    return dict(win=win.astype(BF16), wg=wg.astype(BF16), bg=b_g[None], go=go, wpool=wpool.astype(BF16),
                pscale=pool_scale[None], wo_gla=wo_gla.astype(BF16), wo_pool=w_out[nv:].astype(BF16))


def _odd_proj_body(x_ref, gmix_ref, win_ref, wg_ref, bg_ref, q_ref, k_ref, v_ref, r_ref, u_ref, g_ref):
    xn = _rms(x_ref[...], gmix_ref[...])
    z = _dot(xn, win_ref[...])
    nk = GLA_HEADS * GLA_DK
    nvp = GLA_HEADS * GLA_DVP
    q_ref[...] = z[:, :nk] * (GLA_DK ** -0.5)
    k_ref[...] = z[:, nk:2 * nk]
    v_ref[...] = z[:, 2 * nk:2 * nk + nvp]
    r_ref[...] = z[:, 2 * nk + nvp:2 * nk + 2 * nvp]
    o_u = 2 * nk + 2 * nvp
    u_ref[...] = z[:, o_u:o_u + POOL_DIM]
    zg = _dot(z[:, o_u + POOL_DIM:o_u + POOL_DIM + LANE], wg_ref[...]) + bg_ref[...]
    g_ref[...] = (jnp.minimum(zg, 0.0) - jnp.log1p(jnp.exp(-jnp.abs(zg)))) * (1.0 / GLA_TAU)


def _odd_proj(x, g_mix, ow, tm):
    m, d = x.shape
    nk = GLA_HEADS * GLA_DK
    nvp = GLA_HEADS * GLA_DVP
    row = lambda w: pl.BlockSpec((tm, w), lambda i: (i, 0))
    widths = [nk, nk, nvp, nvp, POOL_DIM, nk]
    return pl.pallas_call(
        _odd_proj_body, grid=(m // tm,),
        in_specs=[row(d), _const_spec((1, d)), _const_spec(ow["win"].shape), _const_spec(ow["wg"].shape),
                  _const_spec((1, nk))],
        out_specs=[row(w) for w in widths],
        out_shape=[jax.ShapeDtypeStruct((m, w), F32) for w in widths],
        compiler_params=_cparams("parallel"), name="odd_proj",
    )(x, g_mix[None], ow["win"], ow["wg"], ow["bg"])


def _gla_out(o, r, go):
    outs = []
    for h in range(GLA_HEADS):
        sl = slice(h * GLA_DVP, (h + 1) * GLA_DVP)
        oh = o[h] if isinstance(o, (list, tuple)) else o[:, sl]
        ms = jnp.sum(oh * oh, axis=-1, keepdims=True) * (1.0 / GLA_DV)
        rh = r[:, sl]
        outs.append(oh * lax.rsqrt(ms + EPS) * go[:, sl] * (rh / (1.0 + jnp.exp(-rh))))
    return jnp.concatenate(outs, axis=-1)


def _gla_scan_body(q_ref, k_ref, v_ref, g_ref, r_ref, go_ref, o_ref, st_ref, st_sc):
    c = pl.program_id(0)

    @pl.when(c == 0)
    def _():
        st_sc[...] = jnp.zeros(st_sc.shape, F32)

    nb, cs = q_ref.shape[:2]
    row = lax.broadcasted_iota(jnp.int32, (cs, cs), 0)
    col = lax.broadcasted_iota(jnp.int32, (cs, cs), 1)
    tril = (col <= row)
    tril_bf = tril.astype(BF16)
    for b in range(nb):
        outs = []
        for h in range(GLA_HEADS):
            ks = slice(h * GLA_DK, (h + 1) * GLA_DK)
            vs = slice(h * GLA_DVP, (h + 1) * GLA_DVP)
            q, k, v, g = q_ref[b, :, ks], k_ref[b, :, ks], v_ref[b, :, vs], g_ref[b, :, ks]
            g_hi = g.astype(BF16)
            g_r = g - g_hi.astype(F32)
            g_mid = g_r.astype(BF16)
            g_lo = (g_r - g_mid.astype(F32)).astype(BF16)
            gc = (jnp.dot(tril_bf, g_hi, preferred_element_type=F32)
                  + jnp.dot(tril_bf, g_mid, preferred_element_type=F32)
                  + jnp.dot(tril_bf, g_lo, preferred_element_type=F32))
            g_last = gc[cs - 1:cs, :]
            qd = q * jnp.exp(gc)
            a = jnp.where(tril, _dot_nt(qd, k * jnp.exp(-gc)), 0.0)
            st = st_sc[b, h]
            outs.append(_dot_nt(qd, st) + _dot(a, v))
            st_sc[b, h] = st * jnp.exp(g_last) + _dot_tn(v, k * jnp.exp(g_last - gc))
        o_ref[b] = _gla_out(outs, r_ref[b], go_ref[...])

    @pl.when(c == pl.num_programs(0) - 1)
    def _():
        st_ref[...] = st_sc[...]


def _gla_scan(q, k, v, g, r, go):
    n, s, nk = q.shape
    nvp = v.shape[2]
    cs = GLA_CHUNK
    blk = lambda w: pl.BlockSpec((n, cs, w), lambda c: (0, c, 0))
    st_shape = (n, GLA_HEADS, GLA_DVP, GLA_DK)
    return pl.pallas_call(
        _gla_scan_body, grid=(s // cs,),
        in_specs=[blk(nk), blk(nk), blk(nvp), blk(nk), blk(nvp), _const_spec((1, nvp))],
        out_specs=[blk(nvp), _const_spec(st_shape)],
        out_shape=[jax.ShapeDtypeStruct((n, s, nvp), F32), jax.ShapeDtypeStruct(st_shape, F32)],
        scratch_shapes=[pltpu.VMEM(st_shape, F32)],
        compiler_params=_cparams("arbitrary"), name="gla_scan",
    )(q, k, v, g, r, go)


def _pool_group_select(vals):
    grp = lax.broadcasted_iota(jnp.int32, (1, POOL_DIM), 1) // POOL_GDIM
    out = vals[-1]
    for gi in range(len(vals) - 2, -1, -1):
        out = jnp.where(grp == gi, vals[gi], out)
    return out


def _pool_body(u_ref, wp_ref, ps_ref, o_ref, ext_sc, *, tm):
    i = pl.program_id(1)
    hist = POOL_HIST + 1

    @pl.when(i == 0)
    def _():
        ext_sc[0:hist, :] = jnp.zeros((hist, POOL_DIM), F32)

    u = u_ref[0]
    ext_sc[hist:hist + tm, :] = u
    acc = u
    sums = []
    for k in range(1, max(POOL_WINDOWS)):
        acc = acc + ext_sc[hist - k:hist - k + tm, :]
        if k + 1 in POOL_WINDOWS:
            sums.append(acc)
    pos = i * tm + lax.broadcasted_iota(jnp.int32, (tm, 1), 0)
    win = _pool_group_select([jnp.full((1, POOL_DIM), w, jnp.int32) for w in POOL_WINDOWS])
    cnt = jnp.minimum(pos + 1, win).astype(F32)
    diff = _pool_group_select(sums) / cnt - u
    o_ref[0] = _dot(diff, wp_ref[...]) * ps_ref[...]
    ext_sc[0:hist, :] = ext_sc[tm:tm + hist, :]


def _pool_prompt(u, wpool, pscale, tm=512):
    n, s, pd = u.shape
    blk = pl.BlockSpec((1, tm, pd), lambda b, i: (b, i, 0))
    return pl.pallas_call(
        functools.partial(_pool_body, tm=tm), grid=(n, s // tm),
        in_specs=[blk, _const_spec((pd, pd)), _const_spec((1, pd))],
        out_specs=blk, out_shape=jax.ShapeDtypeStruct((n, s, pd), F32),
        scratch_shapes=[pltpu.VMEM((tm + POOL_HIST + 1, pd), F32)],
        compiler_params=_cparams("parallel", "arbitrary"), name="pool_prompt",
    )(u, wpool, pscale)


def _pool_sample_body(ue_ref, wp_ref, ps_ref, o_ref, *, past):
    ue = ue_ref[...]
    t = ue.shape[1]
    rowi = lax.broadcasted_iota(jnp.int32, (1, t, 1), 1)
    sums = [jnp.sum(jnp.where(rowi >= t - w, ue, 0.0), axis=1) for w in POOL_WINDOWS]
    cnt = _pool_group_select(
        [jnp.full((1, POOL_DIM), float(min(past + 1, w)), F32) for w in POOL_WINDOWS])
    diff = _pool_group_select(sums) / cnt - ue[:, t - 1, :]
    o_ref[...] = _dot(diff, wp_ref[...]) * ps_ref[...]


def _pool_sample(u_ext, wpool, pscale, past):
    b, t, pd = u_ext.shape
    return pl.pallas_call(
        functools.partial(_pool_sample_body, past=past), grid=(1,),
        in_specs=[_const_spec((b, t, pd)), _const_spec((pd, pd)), _const_spec((1, pd))],
        out_specs=_const_spec((b, pd)), out_shape=jax.ShapeDtypeStruct((b, pd), F32),
        compiler_params=_cparams("arbitrary"), name="pool_sample",
    )(u_ext, wpool, pscale)


def _gla_sample_body(q_ref, k_ref, v_ref, g_ref, r_ref, go_ref, st_ref, o_ref, sn_ref, o_sc, *, bs):
    eye = (lax.broadcasted_iota(jnp.int32, (GLA_DK, GLA_DK), 0)
           == lax.broadcasted_iota(jnp.int32, (GLA_DK, GLA_DK), 1))

    def col(rowv):
        return jnp.sum(jnp.where(eye, rowv, 0.0), axis=1, keepdims=True)

    o_sc[...] = jnp.zeros(o_sc.shape, F32)
    for b in range(bs):
        for h in range(GLA_HEADS):
            ks = slice(h * GLA_DK, (h + 1) * GLA_DK)
            q = q_ref[b:b + 1, ks]
            k = k_ref[b:b + 1, ks]
            eg = jnp.exp(g_ref[b:b + 1, ks])
            v = v_ref[b:b + 1, h * GLA_DVP:h * GLA_DVP + GLA_DV]
            s_new = st_ref[b, h] * col(eg) + col(k) * v
            sn_ref[b, h] = s_new
            o_sc[b:b + 1, h * GLA_DVP:h * GLA_DVP + GLA_DV] = jnp.sum(col(q) * s_new, axis=0, keepdims=True)
    o_ref[...] = _gla_out(o_sc[...], r_ref[...], go_ref[...])


def _gla_sample(q, k, v, g, r, go, state, bs=8):
    b, nk = q.shape
    nvp = v.shape[1]
    row = lambda w: pl.BlockSpec((bs, w), lambda i: (i, 0))
    st = pl.BlockSpec((bs, GLA_HEADS, GLA_DK, GLA_DV), lambda i: (i, 0, 0, 0))
    return pl.pallas_call(
        functools.partial(_gla_sample_body, bs=bs), grid=(b // bs,),
        in_specs=[row(nk), row(nk), row(nvp), row(nk), row(nvp), _const_spec((1, nvp)), st],
        out_specs=[row(nvp), st],
        out_shape=[jax.ShapeDtypeStruct((b, nvp), F32), jax.ShapeDtypeStruct(state.shape, F32)],
        scratch_shapes=[pltpu.VMEM((bs, nvp), F32)],
        compiler_params=_cparams("parallel"), name="gla_sample",
    )(q, k, v, g, r, go, state)


def _sample_q_body(qa_ref, gak_ref, wk_ref, selr_ref, selsw_ref, qabs_ref, qr_ref, qsw_ref):
    qa = qa_ref[...]
    lane = lax.broadcasted_iota(jnp.int32, (1, qa.shape[1]), 1)
    qg = qa.astype(F32) * gak_ref[...]
    for h in range(MLA_HEADS):
        base = MLA_PAIR * (h // 2) + MLA_NOPE * (h % 2)
        nope = (lane >= base) & (lane < base + MLA_NOPE)
        qabs_ref[h] = _dot_nt(jnp.where(nope, qg, 0.0), wk_ref[...]).astype(BF16)
        qr_ref[h] = jnp.dot(qa, selr_ref[h], preferred_element_type=F32).astype(BF16)
        qsw_ref[h] = jnp.dot(qa, selsw_ref[h], preferred_element_type=F32).astype(BF16)


def _sample_q(qa, gak, wk):
    b, width = qa.shape
    cols = _mla_cols()
    selr = np.zeros((MLA_HEADS, width, MLA_ROPE), np.float32)
    selsw = np.zeros((MLA_HEADS, width, MLA_ROPE), np.float32)
    half = MLA_ROPE // 2
    for h in range(MLA_HEADS):
        rc = cols[h, MLA_NOPE:]
        selr[h, rc, np.arange(MLA_ROPE)] = 1.0
        selsw[h, rc[half:], np.arange(half)] = 1.0
        selsw[h, rc[:half], half + np.arange(half)] = -1.0
    outs = pl.pallas_call(
        _sample_q_body, grid=(1,),
        in_specs=[_const_spec(qa.shape), _const_spec(gak.shape), _const_spec(wk.shape),
                  _const_spec(selr.shape), _const_spec(selsw.shape)],
        out_specs=[_const_spec((MLA_HEADS, b, MLA_KV_LORA)), _const_spec((MLA_HEADS, b, MLA_ROPE)),
                   _const_spec((MLA_HEADS, b, MLA_ROPE))],
        out_shape=[jax.ShapeDtypeStruct((MLA_HEADS, b, MLA_KV_LORA), BF16),
                   jax.ShapeDtypeStruct((MLA_HEADS, b, MLA_ROPE), BF16),
                   jax.ShapeDtypeStruct((MLA_HEADS, b, MLA_ROPE), BF16)],
        compiler_params=_cparams("arbitrary"), name="sample_q",
    )(qa, gak, wk, jnp.asarray(selr, BF16), jnp.asarray(selsw, BF16))
    return [jnp.swapaxes(o, 0, 1) for o in outs]


def _paged_body(pt_ref, ckv_hbm, kr_hbm, k_hbm, v_hbm,
                qabs_ref, qr_ref, qsw_ref, qb_ref, ckvn_ref, krn_ref, tc_ref, ts_ref, tcn_ref, tsn_ref,
                wuk_ref, e64_ref, olat_ref, part_ref,
                ckv_buf, kr_buf, k_buf, v_buf, sem, m_sc, l_sc, acc_sc, qcol_sc, *, npg, layer):
    i = pl.program_id(0)
    c = pl.program_id(1)
    nc = pl.num_programs(1)
    step = i * nc + c
    slot = step % 2
    caches = ((ckv_hbm, ckv_buf), (kr_hbm, kr_buf), (k_hbm, k_buf), (v_hbm, v_buf))

    def page_copies(ii, cc, sl):
        out = []
        for j in range(npg):
            pg = pt_ref[ii, cc * npg + j]
            for a, (hbm, buf) in enumerate(caches):
                out.append(pltpu.make_async_copy(hbm.at[layer, pg], buf.at[sl, j], sem.at[sl, a]))
        return out

    @pl.when(step == 0)
    def _():
        for cp in page_copies(i, c, slot):
            cp.start()

    @pl.when(step + 1 < pl.num_programs(0) * nc)
    def _():
        wrap = c + 1 == nc
        for cp in page_copies(jnp.where(wrap, i + 1, i), jnp.where(wrap, 0, c + 1), 1 - slot):
            cp.start()

    for cp in page_copies(i, c, slot):
        cp.wait()

    class _Page:
        def __init__(self, buf, j):
            self.buf, self.j = buf, j

        def __getitem__(self, idx):
            if idx is Ellipsis:
                return self.buf[slot, self.j]
            return self.buf[slot, self.j, idx]

    pages = [_Page(buf, j) for j in range(npg) for _, buf in caches]
    rows = MOBA_HEADS
    eye = (lax.broadcasted_iota(jnp.int32, (MOBA_HD, LANE), 0)
           == lax.broadcasted_iota(jnp.int32, (MOBA_HD, LANE), 1))

    @pl.when(c == 0)
    def _():
        m_sc[...] = jnp.full(m_sc.shape, NEG_INF, F32)
        l_sc[...] = jnp.zeros(l_sc.shape, F32)
        acc_sc[...] = jnp.zeros(acc_sc.shape, F32)
        for h in range(MOBA_HEADS):
            qcol_sc[h] = jnp.broadcast_to(
                jnp.sum(jnp.where(eye, qb_ref[0, h:h + 1, :], 0.0), axis=1, keepdims=True), (MOBA_HD, LANE))

    qr, qsw = qr_ref[0], qsw_ref[0]

    def mla_update(ckv_b, kr_t, tc_t, ts_t, n_valid):
        kn = jnp.dot(ckv_b, wuk_ref[...], preferred_element_type=F32)
        mix = (jnp.dot(ckv_b, qabs_ref[0], preferred_element_type=F32)
               + jnp.dot((kn * kn).astype(BF16), e64_ref[...], preferred_element_type=F32))
        mix_t = mix.T
        ss = mix_t[rows:2 * rows] + jnp.sum(kr_t * kr_t, axis=0, keepdims=True)
        s = mix_t[0:rows] + _dot(qr, kr_t * tc_t) + _dot(qsw, kr_t * ts_t)
        s = s * lax.rsqrt(ss * (1.0 / MLA_QK) + EPS)
        if n_valid is not None:
            s = jnp.where(lax.broadcasted_iota(jnp.int32, s.shape, 1) < n_valid, s, NEG_INF)
        m_prev = m_sc[...]
        m_new = jnp.maximum(m_prev, jnp.max(s, axis=-1, keepdims=True))
        alpha = jnp.exp(m_prev - m_new)
        p = jnp.exp(s - m_new)
        l_sc[...] = alpha * l_sc[...] + jnp.sum(p, axis=-1, keepdims=True)
        acc_sc[...] = alpha * acc_sc[...] + jnp.dot(p.astype(BF16), ckv_b, preferred_element_type=F32)
        m_sc[...] = m_new

    mla_update(jnp.concatenate([pages[4 * j][...].astype(BF16) for j in range(npg)], axis=0),
               jnp.concatenate([pages[4 * j + 1][...] for j in range(npg)], axis=1),
               tc_ref[...], ts_ref[...], None)

    rowi = lax.broadcasted_iota(jnp.int32, (rows, LANE), 0)
    lanei = lax.broadcasted_iota(jnp.int32, (rows, LANE), 1)
    ppb = MOBA_BLOCK // PAGE_SIZE
    for g in range(npg // ppb):
        pg = [g * ppb + t for t in range(ppb)]
        raw = []
        for j in pg:
            s8 = jnp.zeros((rows, LANE), F32)
            for h in range(MOBA_HEADS):
                sh = jnp.sum(pages[4 * j + 2][h] * qcol_sc[h], axis=0, keepdims=True)
                s8 = jnp.where(rowi == h, sh, s8)
            raw.append(s8)
        gate = sum(jnp.sum(r, axis=-1, keepdims=True) for r in raw) * (1.0 / MOBA_BLOCK)
        sc = [r * (MOBA_HD ** -0.5) for r in raw]
        m = functools.reduce(jnp.maximum, [jnp.max(t, axis=-1, keepdims=True) for t in sc])
        p = [jnp.exp(t - m) for t in sc]
        l = sum(jnp.sum(t, axis=-1, keepdims=True) for t in p)
        o8 = jnp.zeros((rows, LANE), F32)
        for h in range(MOBA_HEADS):
            pv = sum(p[t][h:h + 1, :] * pages[4 * j + 3][h] for t, j in enumerate(pg))
            ocol = jnp.sum(pv, axis=1, keepdims=True)
            o8 = jnp.where(rowi == h, jnp.sum(jnp.where(eye, ocol, 0.0), axis=0, keepdims=True), o8)
        part_ref[0, g] = jnp.where(lanei == MOBA_HD, m,
                                   jnp.where(lanei == MOBA_HD + 1, l,
                                             jnp.where(lanei == MOBA_HD + 2, gate, o8)))

    @pl.when(c == pl.num_programs(1) - 1)
    def _():
        mla_update(ckvn_ref[0].astype(BF16), krn_ref[0], tcn_ref[...], tsn_ref[...], 1)
        olat_ref[0] = acc_sc[...] / l_sc[...]


def _paged_attend(page_table, layer, c_ckv, c_kr, c_k, c_v, qabs, qr, qsw, qb, ckv_new, kr_new, gk_rope,
                  wuk, npg=16):
    b, n_pages = page_table.shape
    past = n_pages * PAGE_SIZE
    rows = MOBA_HEADS
    nstep = n_pages // npg
    ppb = MOBA_BLOCK // PAGE_SIZE
    c, s = _rope_tables(jnp.arange(past + 1), MLA_ROPE // 2, MLA_THETA)
    tc = (jnp.concatenate([c, c], -1) * gk_rope[None]).T
    ts = (jnp.concatenate([s, s], -1) * gk_rope[None]).T
    e64 = np.zeros((MLA_HEADS * MLA_NOPE, LANE), np.float32)
    for h in range(MLA_HEADS):
        e64[h * MLA_NOPE:(h + 1) * MLA_NOPE, rows + h] = 1.0
    qabs_t = jnp.pad(jnp.swapaxes(qabs, 1, 2), ((0, 0), (0, 0), (0, LANE - rows)))

    caches = [c_ckv, c_kr, c_k, c_v]
    per_sample = lambda r, w: pl.BlockSpec((1, r, w), lambda i, cc, pt: (i, 0, 0))
    const = lambda shape: pl.BlockSpec(shape, lambda i, cc, pt: (0,) * len(shape))
    tab = pl.BlockSpec((MLA_ROPE, npg * PAGE_SIZE), lambda i, cc, pt: (0, cc))
    in_specs = [pl.BlockSpec(memory_space=pl.ANY)] * len(caches) + [
        per_sample(MLA_KV_LORA, LANE), per_sample(rows, MLA_ROPE), per_sample(rows, MLA_ROPE),
        per_sample(rows, LANE), per_sample(PAGE_SIZE, MLA_KV_LORA), per_sample(MLA_ROPE, PAGE_SIZE),
        tab, tab, const((MLA_ROPE, PAGE_SIZE)), const((MLA_ROPE, PAGE_SIZE)), const(wuk.shape),
        const(e64.shape)]
    grid_spec = pltpu.PrefetchScalarGridSpec(
        num_scalar_prefetch=1, grid=(b, nstep), in_specs=in_specs,
        out_specs=[per_sample(rows, MLA_KV_LORA),
                   pl.BlockSpec((1, npg // ppb, rows, LANE), lambda i, cc, pt: (i, cc, 0, 0))],
        scratch_shapes=[pltpu.VMEM((2, npg) + t.shape[2:], t.dtype) for t in caches]
        + [pltpu.SemaphoreType.DMA((2, len(caches))),
           pltpu.VMEM((rows, 1), F32), pltpu.VMEM((rows, 1), F32),
           pltpu.VMEM((rows, MLA_KV_LORA), F32),
           pltpu.VMEM((MOBA_HEADS, MOBA_HD, LANE), F32)])
    pad_rows = lambda t: jnp.pad(t[:, None, :], ((0, 0), (0, PAGE_SIZE - 1), (0, 0)))
    pad_cols = lambda t: jnp.pad(t[:, :, None], ((0, 0), (0, 0), (0, PAGE_SIZE - 1)))
    tnew = lambda t: jnp.broadcast_to(t[:, past:past + 1], (MLA_ROPE, PAGE_SIZE))
    qb_pad = jnp.pad(qb.reshape(b, MOBA_HEADS, MOBA_HD), ((0, 0), (0, 0), (0, LANE - MOBA_HD)))
    return pl.pallas_call(
        functools.partial(_paged_body, npg=npg, layer=layer), grid_spec=grid_spec,
        out_shape=[jax.ShapeDtypeStruct((b, rows, MLA_KV_LORA), F32),
                   jax.ShapeDtypeStruct((b, n_pages // ppb, rows, LANE), F32)],
        compiler_params=_cparams("arbitrary", "arbitrary"), name="paged_attend",
    )(page_table, *caches, qabs_t, qr, qsw, qb_pad, pad_rows(ckv_new), pad_cols(kr_new),
      tc[:, :past], ts[:, :past], tnew(tc), tnew(ts), wuk, jnp.asarray(e64, BF16))


def _moba_combine_body(part_ref, q_ref, kn_ref, vn_ref, out_ref):
    part = part_ref[0]
    o = part[:, :, :MOBA_HD]
    m = part[:, :, MOBA_HD:MOBA_HD + 1]
    l = part[:, :, MOBA_HD + 1:MOBA_HD + 2]
    gate = part[:, :, MOBA_HD + 2:MOBA_HD + 3]
    sel = _top_blocks(gate, jnp.full(gate.shape, True), 0)
    s_own = jnp.sum(q_ref[0] * kn_ref[0], axis=-1, keepdims=True) * (MOBA_HD ** -0.5)
    m_sel = jnp.where(sel, m, NEG_INF)
    mt = jnp.maximum(jnp.max(m_sel, axis=0), s_own)
    w = jnp.exp(m_sel - mt)
    w_own = jnp.exp(s_own - mt)
    den = jnp.sum(w * l, axis=0) + w_own
    num = jnp.sum(w * o, axis=0) + w_own * vn_ref[0]
    out_ref[0] = num / den


def _moba_combine(part, qb, kb_new, vb_new):
    b, nb = part.shape[:2]
    hb = MOBA_HEADS * MOBA_HD
    r3 = lambda t: t.reshape(b, MOBA_HEADS, MOBA_HD)
    blk = pl.BlockSpec((1, MOBA_HEADS, MOBA_HD), lambda i: (i, 0, 0))
    out = pl.pallas_call(
        _moba_combine_body, grid=(b,),
        in_specs=[pl.BlockSpec((1, nb, MOBA_HEADS, LANE), lambda i: (i, 0, 0, 0)), blk, blk, blk],
        out_specs=blk, out_shape=jax.ShapeDtypeStruct((b, MOBA_HEADS, MOBA_HD), F32),
        compiler_params=_cparams("parallel"), name="moba_combine",
    )(part, r3(qb), r3(kb_new), r3(vb_new))
    return out.reshape(b, hb)


def _latent_out_body(olat_ref, wv_ref, o_ref):
    lane = lax.broadcasted_iota(jnp.int32, (1, MLA_HEADS * MLA_V), 1)
    wv = wv_ref[...]
    acc = jnp.zeros(o_ref.shape, F32)
    for h in range(MLA_HEADS):
        wh = jnp.where((lane >= h * MLA_V) & (lane < (h + 1) * MLA_V), wv, jnp.zeros_like(wv))
        acc = acc + _dot(olat_ref[h], wh)
    o_ref[...] = acc


def _latent_out(olat_hm, wv):
    _, b, _ = olat_hm.shape
    return pl.pallas_call(
        _latent_out_body, grid=(1,),
        in_specs=[_const_spec(olat_hm.shape), _const_spec(wv.shape)],
        out_specs=_const_spec((b, MLA_HEADS * MLA_V)),
        out_shape=jax.ShapeDtypeStruct((b, MLA_HEADS * MLA_V), F32),
        compiler_params=_cparams("arbitrary"), name="latent_out",
    )(olat_hm, wv)


def _even_layer(yp, ys, page_table, layer, c_ckv, c_kr, c_k, c_v, g_mix, w_in, g_cq, g_ckv, w_uq, w_uk, w_uv,
                g_aq, g_ak, g_bq, g_bk, w_out):
    n, s, d = yp.shape
    b = ys.shape[0]
    past = page_table.shape[1] * PAGE_SIZE
    hb = MOBA_HEADS * MOBA_HD
    ew = _even_weights(w_in, g_cq, g_ckv, w_uq, w_uk, w_uv, g_aq, g_ak, g_bq, g_bk)
    wo_a = w_out[:MLA_HEADS * MLA_V].astype(BF16)
    wo_b = w_out[MLA_HEADS * MLA_V:].astype(BF16)

    ckv, kr, qa, ka, va, qb, kb, vb, km = _even_proj(yp.reshape(n * s, d), g_mix, ew, jnp.arange(s),
                                                     MOBA_BLOCK)
    r3 = lambda t: t.reshape(n, s, t.shape[-1])
    oa = _mla_attn(r3(qa), r3(ka), r3(va))
    ob = _moba_attn(r3(qb), r3(kb), r3(vb), km.reshape(n, s // MOBA_BLOCK, hb))
    yp_new = _mm2_res(oa.reshape(n * s, -1), ob.reshape(n * s, -1), wo_a, wo_b, yp.reshape(n * s, d),
                      512).reshape(n, s, d)
    p_caches = (r3(ckv), r3(kr), kb.reshape(n, s, MOBA_HEADS, MOBA_HD), vb.reshape(n, s, MOBA_HEADS, MOBA_HD))

    sckv, skr, sqa, _, _, sqb, skb, svb, _ = _even_proj(
        ys.reshape(b, d), g_mix, ew, jnp.full((b,), past, jnp.int32), b)
    qabs, qr, qsw = _sample_q(sqa, ew["gak"], ew["wk"])
    wuk = w_uk.reshape(MLA_KV_LORA, -1).astype(BF16)
    olat, part = _paged_attend(page_table, layer, c_ckv, c_kr, c_k, c_v, qabs, qr, qsw, sqb, sckv, skr,
                               g_ak[MLA_NOPE:], wuk)
    soa = _latent_out(jnp.swapaxes(olat, 0, 1), ew["wv"])
    sob = _moba_combine(part, sqb, skb, svb)
    ys_new = _mm2_res(soa, sob, wo_a, wo_b, ys.reshape(b, d), b).reshape(b, 1, d)
    s_caches = (sckv[:, None], skr[:, None], skb.reshape(b, 1, MOBA_HEADS, MOBA_HD),
                svb.reshape(b, 1, MOBA_HEADS, MOBA_HD))
    return yp_new, ys_new, p_caches, s_caches


def _odd_layer(yp, ys, past, s_prev, u_prev, g_mix, w_in, w_g, b_g, g_o, w_pool, pool_scale, w_out):
    n, s, d = yp.shape
    b = ys.shape[0]
    ow = _odd_weights(w_in, w_g, b_g, g_o, w_pool, pool_scale, w_out)

    q, k, v, r, u, g = _odd_proj(yp.reshape(n * s, d), g_mix, ow, 512)
    r3 = lambda t: t.reshape(n, s, t.shape[-1])
    og, st_t = _gla_scan(r3(q), r3(k), r3(v), r3(g), r3(r), ow["go"])
    u3 = r3(u)
    pm = _pool_prompt(u3, ow["wpool"], ow["pscale"])
    yp_new = _mm2_res(og.reshape(n * s, -1), pm.reshape(n * s, -1), ow["wo_gla"], ow["wo_pool"],
                      yp.reshape(n * s, d), 512).reshape(n, s, d)
    p_state = jnp.swapaxes(st_t, 2, 3)[..., :GLA_DV]
    p_pool = u3[:, s - POOL_HIST:]

    sq, sk, sv, sr, su, sg = _odd_proj(ys.reshape(b, d), g_mix, ow, b)
    sog, s_new = _gla_sample(sq, sk, sv, sg, sr, ow["go"], s_prev)
    u_ext = jnp.concatenate([u_prev, su[:, None]], axis=1)
    spm = _pool_sample(u_ext, ow["wpool"], ow["pscale"], past)
    ys_new = _mm2_res(sog, spm, ow["wo_gla"], ow["wo_pool"], ys.reshape(b, d), b).reshape(b, 1, d)
    return yp_new, ys_new, (p_state, p_pool), (s_new, u_ext[:, 1:])


def kernel(x_prompt, x_sample, cache_mla_ckv, cache_mla_krope, cache_moba_k, cache_moba_v, state_gla, state_pool, cache_mem_k, cache_mem_v, page_table, mem_prompt, g_mix, g_memx, g_memm, w_mq, w_mk, w_mv, g_mq, g_mk, w_mo, g_ffn, w_ff1, w_ff2, ev_w_in, ev_g_cq, ev_g_ckv, ev_w_uq, ev_w_uk, ev_w_uv, ev_g_mla_q, ev_g_mla_k, ev_g_moba_q, ev_g_moba_k, ev_w_out, od_w_in, od_w_gate, od_b_gate, od_g_gla, od_w_pool, od_pool_scale, od_w_out):
    depth = g_mix.shape[0]
    n, s, d = x_prompt.shape
    b = x_sample.shape[0]
    past = page_table.shape[1] * PAGE_SIZE
    yp, ys = x_prompt, x_sample
    outs = {name: [] for name in ("p_ckv", "p_kr", "p_k", "p_v", "s_ckv", "s_kr", "s_k", "s_v",
                                  "p_gla", "p_pool", "s_gla", "s_pool", "p_mk", "p_mv")}
    for l in range(depth):
        i = l // 2
        if l % 2 == 0:
            yp, ys, pc, sc = _even_layer(
                yp, ys, page_table, i, cache_mla_ckv, jnp.transpose(cache_mla_krope, (0, 1, 3, 2)),
                jnp.transpose(cache_moba_k, (0, 1, 3, 4, 2)), jnp.transpose(cache_moba_v, (0, 1, 3, 4, 2)),
                g_mix[l], ev_w_in[i], ev_g_cq[i], ev_g_ckv[i], ev_w_uq[i], ev_w_uk[i], ev_w_uv[i],
                ev_g_mla_q[i], ev_g_mla_k[i], ev_g_moba_q[i], ev_g_moba_k[i], ev_w_out[i])
            for name, val in zip(("p_ckv", "p_kr", "p_k", "p_v"), pc):
                outs[name].append(val)
            for name, val in zip(("s_ckv", "s_kr", "s_k", "s_v"), sc):
                outs[name].append(val)
        else:
            yp, ys, ps, ss = _odd_layer(
                yp, ys, past, state_gla[i], state_pool[i], g_mix[l], od_w_in[i], od_w_gate[i], od_b_gate[i],
                od_g_gla[i], od_w_pool[i], od_pool_scale[i], od_w_out[i])
            for name, val in zip(("p_gla", "p_pool"), ps):
                outs[name].append(val)
            for name, val in zip(("s_gla", "s_pool"), ss):
                outs[name].append(val)
        mk, mv = _mem_kv(mem_prompt, g_memm[l], w_mk[l], w_mv[l], g_mk[l])
        ml = mk.shape[1]
        outs["p_mk"].append(mk.reshape(n, ml, MEM_HEADS, MEM_HD))
        outs["p_mv"].append(mv.reshape(n, ml, MEM_HEADS, MEM_HD))
        yp = _mem_attn(yp, g_memx[l], w_mq[l], g_mq[l], mk, mv, w_mo[l], 512)
        ys = _mem_attn_sample(ys.reshape(b, d), g_memx[l], w_mq[l], g_mq[l], cache_mem_k, cache_mem_v, l,
                              w_mo[l]).reshape(b, 1, d)
        w1 = w_ff1[l].astype(BF16)
        w2 = w_ff2[l].astype(BF16)
        yp = _ffn(yp.reshape(n * s, d), g_ffn[l], w1, w2, 1024).reshape(n, s, d)
        ys = _ffn(ys.reshape(b, d), g_ffn[l], w1, w2, b).reshape(b, 1, d)
    st = lambda name: jnp.stack(outs[name])
    return (yp, ys, st("p_ckv"), st("p_kr"), st("p_k"), st("p_v"), st("p_gla"), st("p_pool"),
            st("p_mk"), st("p_mv"), st("s_ckv"), st("s_kr"), st("s_k"), st("s_v"), st("s_gla"), st("s_pool"))
```

```python
import functools
import math

import jax
import jax.numpy as jnp
import numpy as np
from jax import lax
from jax.experimental import pallas as pl
from jax.experimental.pallas import tpu as pltpu

F32 = jnp.float32
BF16 = jnp.bfloat16
NEG_INF = float("-inf")

EPS = 1e-6
D_MODEL = 1024
PAGE_SIZE = 128
MLA_HEADS = 8
MLA_NOPE = 64
MLA_ROPE = 32
MLA_QK = MLA_NOPE + MLA_ROPE
MLA_V = 64
MLA_Q_LORA = 384
MLA_KV_LORA = 256
MLA_THETA = 10000.0
MLA_PAIR = 256
MOBA_HEADS = 8
MOBA_HD = 64
MOBA_ROT = MOBA_HD // 4
MOBA_BLOCK = 256
MOBA_TOPK = 3
ROPE_THETA = 500000.0
GLA_HEADS = 4
GLA_DK = 128
GLA_DV = 192
GLA_DVP = 256
GLA_GATE_RANK = 16
GLA_TAU = 16.0
GLA_CHUNK = 64
POOL_WINDOWS = (2, 4, 8, 16)
POOL_GDIM = 64
POOL_DIM = POOL_GDIM * len(POOL_WINDOWS)
POOL_HIST = max(POOL_WINDOWS) - 1
MEM_HEADS = 4
MEM_HD = 128
LANE = 128
VMEM_LIMIT = 56 * 1024 * 1024


def _rms(x, g):
    return x * lax.rsqrt(jnp.mean(x * x, axis=-1, keepdims=True) + EPS) * g


def _dot(a, b):
    return jnp.dot(a.astype(BF16), b.astype(BF16), preferred_element_type=F32)


def _dot_nt(a, b):
    return lax.dot_general(a.astype(BF16), b.astype(BF16), (((1,), (1,)), ((), ())),
                           preferred_element_type=F32)


def _dot_tn(a, b):
    return lax.dot_general(a.astype(BF16), b.astype(BF16), (((0,), (0,)), ((), ())),
                           preferred_element_type=F32)


def _split(a):
    hi = a.astype(BF16)
    lo = (a - hi.astype(F32)).astype(BF16)
    return hi, lo


def _dot_hl(a, sel):
    hi, lo = _split(a)
    return (jnp.dot(hi, sel, preferred_element_type=F32)
            + jnp.dot(lo, sel, preferred_element_type=F32))


def _cparams(*sem):
    return pltpu.CompilerParams(dimension_semantics=sem, vmem_limit_bytes=VMEM_LIMIT)


def _const_spec(shape):
    nd = len(shape)
    return pl.BlockSpec(shape, lambda *_: (0,) * nd)


def _mla_cols():
    cols = np.zeros((MLA_HEADS, MLA_QK), np.int32)
    for h in range(MLA_HEADS):
        base = MLA_PAIR * (h // 2)
        cols[h, :MLA_NOPE] = base + MLA_NOPE * (h % 2) + np.arange(MLA_NOPE)
        cols[h, MLA_NOPE:] = base + 2 * MLA_NOPE + MLA_ROPE * (h % 2) + np.arange(MLA_ROPE)
    return cols


def _rope_tables(pos, half, theta):
    inv = jnp.exp(jnp.arange(half, dtype=F32) * (-math.log(theta) / half))
    ang = pos.astype(F32)[:, None] * inv[None, :]
    return jnp.cos(ang), jnp.sin(ang)


def _mla_pair_tables(pos):
    c, s = _rope_tables(pos, MLA_ROPE // 2, MLA_THETA)
    p = pos.shape[0]
    one = jnp.ones((p, 2 * MLA_NOPE), F32)
    pad1 = jnp.ones((p, MLA_PAIR - 2 * MLA_NOPE - 2 * MLA_ROPE), F32)
    ct = jnp.concatenate([one, c, c, c, c, pad1], -1)
    st = jnp.concatenate([0 * one, s, s, s, s, 0 * pad1], -1)
    return ct, st


def _moba_pair_tables(pos):
    c, s = _rope_tables(pos, MOBA_ROT // 2, ROPE_THETA)
    p = pos.shape[0]
    one = jnp.ones((p, MOBA_HD - MOBA_ROT), F32)
    ch = jnp.concatenate([c, c, one], -1)
    sh = jnp.concatenate([s, s, 0 * one], -1)
    return jnp.concatenate([ch, ch], -1), jnp.concatenate([sh, sh], -1)


def _even_weights(w_in, g_cq, g_ckv, w_uq, w_uk, w_uv, g_aq, g_ak, g_bq, g_bk):
    i0 = MLA_Q_LORA + MLA_KV_LORA
    i1 = i0 + MLA_ROPE
    idx = np.concatenate([np.arange(0, i0), np.arange(i1, w_in.shape[1]), np.arange(i0, i1)])
    win = jnp.pad(w_in[:, idx], ((0, 0), (0, LANE - MLA_ROPE))).astype(BF16)
    cols = _mla_cols()
    width = MLA_PAIR * MLA_HEADS // 2
    wq = jnp.zeros((MLA_Q_LORA, width), F32).at[:, cols.reshape(-1)].set(
        w_uq.reshape(MLA_Q_LORA, -1)).astype(BF16)
    wk = jnp.zeros((MLA_KV_LORA, width), F32).at[:, cols[:, :MLA_NOPE].reshape(-1)].set(
        w_uk.reshape(MLA_KV_LORA, -1)).astype(BF16)
    wv = w_uv.reshape(MLA_KV_LORA, -1).astype(BF16)
    pk = np.zeros((MLA_ROPE, width), np.float32)
    e = np.zeros((width, LANE), np.float32)
    for h in range(MLA_HEADS):
        pk[np.arange(MLA_ROPE), cols[h, MLA_NOPE:]] = 1.0
        e[cols[h], h] = 1.0
    gaq = jnp.zeros((1, width), F32).at[0, cols.reshape(-1)].set(jnp.tile(g_aq, MLA_HEADS))
    gak = jnp.zeros((1, width), F32).at[0, cols.reshape(-1)].set(jnp.tile(g_ak, MLA_HEADS))
    b64 = np.kron(np.eye(MOBA_HEADS, dtype=np.float32), np.ones((MOBA_HD, MOBA_HD), np.float32))
    return dict(
        win=win, gcq=g_cq[None], gckv=g_ckv[None], wq=wq, wk=wk, wv=wv,
        pk=jnp.asarray(pk, BF16), e=jnp.asarray(e, BF16), et=jnp.asarray(e.T, BF16),
        gaq=gaq, gak=gak, b64=jnp.asarray(b64, BF16),
        gbq=jnp.tile(g_bq, MOBA_HEADS)[None], gbk=jnp.tile(g_bk, MOBA_HEADS)[None])


def _even_proj_body(x_ref, gmix_ref, win_ref, gcq_ref, gckv_ref, wq_ref, wk_ref, wv_ref, pk_ref,
                    e_ref, et_ref, gaq_ref, gak_ref, ca_ref, sa_ref, b64_ref, gbq_ref, gbk_ref,
                    cb_ref, sb_ref,
                    ckv_ref, kr_ref, qa_ref, ka_ref, va_ref, qb_ref, kb_ref, vb_ref, km_ref):
    xn = _rms(x_ref[...], gmix_ref[...])
    z = _dot(xn, win_ref[...])
    o_ckv = MLA_Q_LORA
    o_qb = o_ckv + MLA_KV_LORA
    hb = MOBA_HEADS * MOBA_HD
    cqn = _rms(z[:, :o_ckv], gcq_ref[...])
    ckv = _rms(z[:, o_ckv:o_qb], gckv_ref[...])
    qb = z[:, o_qb:o_qb + hb]
    kb = z[:, o_qb + hb:o_qb + 2 * hb]
    vb = z[:, o_qb + 2 * hb:o_qb + 3 * hb]
    kr = z[:, o_qb + 3 * hb:o_qb + 3 * hb + MLA_ROPE]
    ckv_ref[...] = ckv
    kr_ref[...] = kr
    vb_ref[...] = vb

    npair = MLA_HEADS // 2
    ca = jnp.concatenate([ca_ref[...]] * npair, axis=-1)
    sa = jnp.concatenate([sa_ref[...]] * npair, axis=-1)
    width = MLA_PAIR * npair
    lane = lax.broadcasted_iota(jnp.int32, (1, width), 1)
    in_pair = lane % MLA_PAIR
    first_half = ((in_pair >= 2 * MLA_NOPE) & (in_pair < 2 * MLA_NOPE + 2 * MLA_ROPE)
                  & (lane % MLA_ROPE < MLA_ROPE // 2))

    def norm_rope_a(t, g):
        ss = _dot_hl(t * t, e_ref[...])
        rs = lax.rsqrt(ss * (1.0 / MLA_QK) + EPS)
        tn = t * _dot_hl(rs, et_ref[...]) * g
        sw = jnp.where(first_half, -pltpu.roll(tn, width - MLA_ROPE // 2, 1),
                       pltpu.roll(tn, MLA_ROPE // 2, 1))
        return tn * ca + sw * sa

    qa = norm_rope_a(_dot(cqn, wq_ref[...]), gaq_ref[...]) * (MLA_QK ** -0.5)
    ka = norm_rope_a(_dot(ckv, wk_ref[...]) + _dot_hl(kr, pk_ref[...]), gak_ref[...])
    qa_ref[...] = qa.astype(BF16)
    ka_ref[...] = ka.astype(BF16)
    va_ref[...] = _dot(ckv, wv_ref[...]).astype(BF16)

    cb = jnp.concatenate([cb_ref[...]] * (MOBA_HEADS // 2), axis=-1)
    sb = jnp.concatenate([sb_ref[...]] * (MOBA_HEADS // 2), axis=-1)
    lane_b = lax.broadcasted_iota(jnp.int32, (1, hb), 1)
    first_half_b = lane_b % MOBA_HD < MOBA_ROT // 2

    def norm_rope_b(t, g):
        ss = _dot_hl(t * t, b64_ref[...])
        tn = t * lax.rsqrt(ss * (1.0 / MOBA_HD) + EPS) * g
        sw = jnp.where(first_half_b, -pltpu.roll(tn, hb - MOBA_ROT // 2, 1),
                       pltpu.roll(tn, MOBA_ROT // 2, 1))
        return tn * cb + sw * sb

    qb_ref[...] = norm_rope_b(qb, gbq_ref[...])
    kbn = norm_rope_b(kb, gbk_ref[...])
    kb_ref[...] = kbn
    km_ref[0] = jnp.mean(kbn, axis=0, keepdims=True)


def _even_proj(x, g_mix, ew, pos, tm):
    m, d = x.shape
    nt = m // tm
    ca, sa = _mla_pair_tables(pos)
    cb, sb = _moba_pair_tables(pos)
    npt = pos.shape[0] // tm
    row = lambda w: pl.BlockSpec((tm, w), lambda i: (i, 0))
    tab = lambda w: pl.BlockSpec((tm, w), lambda i: (i % npt, 0))
    hb = MOBA_HEADS * MOBA_HD
    wa = MLA_PAIR * MLA_HEADS // 2
    consts = [g_mix[None], ew["win"], ew["gcq"], ew["gckv"], ew["wq"], ew["wk"], ew["wv"], ew["pk"],
              ew["e"], ew["et"], ew["gaq"], ew["gak"]]
    consts2 = [ew["b64"], ew["gbq"], ew["gbk"]]
    in_specs = ([row(d)] + [_const_spec(c.shape) for c in consts] + [tab(MLA_PAIR), tab(MLA_PAIR)]
                + [_const_spec(c.shape) for c in consts2] + [tab(LANE), tab(LANE)])
    out_shape = [
        jax.ShapeDtypeStruct((m, MLA_KV_LORA), F32), jax.ShapeDtypeStruct((m, MLA_ROPE), F32),
        jax.ShapeDtypeStruct((m, wa), BF16), jax.ShapeDtypeStruct((m, wa), BF16),
        jax.ShapeDtypeStruct((m, MLA_HEADS * MLA_V), BF16),
        jax.ShapeDtypeStruct((m, hb), F32), jax.ShapeDtypeStruct((m, hb), F32),
        jax.ShapeDtypeStruct((m, hb), F32), jax.ShapeDtypeStruct((nt, 1, hb), F32)]
    out_specs = [row(MLA_KV_LORA), row(MLA_ROPE), row(wa), row(wa), row(MLA_HEADS * MLA_V),
                 row(hb), row(hb), row(hb), pl.BlockSpec((1, 1, hb), lambda i: (i, 0, 0))]
    return pl.pallas_call(
        _even_proj_body, grid=(nt,), in_specs=in_specs, out_specs=out_specs, out_shape=out_shape,
        compiler_params=_cparams("parallel"), name="even_proj",
    )(x, *consts, ca, sa, *consts2, cb, sb)


def _mla_head_mask(e):
    lane = lax.broadcasted_iota(jnp.int32, (1, MLA_PAIR), 1)
    nope = (lane >= MLA_NOPE * e) & (lane < MLA_NOPE * (e + 1))
    r0 = 2 * MLA_NOPE + MLA_ROPE * e
    return nope | ((lane >= r0) & (lane < r0 + MLA_ROPE))


def _mla_attn_body(q_ref, k_ref, v_ref, o_ref, s_sc, mrun_sc, lrun_sc, acc_sc, *, tq):
    qi = pl.program_id(2)
    q = q_ref[0]
    qs = [jnp.where(_mla_head_mask(e), q, jnp.zeros_like(q)) for e in (0, 1)]
    causal = (lax.broadcasted_iota(jnp.int32, (tq, tq), 1)
              <= lax.broadcasted_iota(jnp.int32, (tq, tq), 0))
    mrun_sc[...] = jnp.full(mrun_sc.shape, NEG_INF, F32)

    def scores(ki, diagonal):
        k = k_ref[0, pl.ds(pl.multiple_of(ki * tq, tq), tq), :]
        for e in (0, 1):
            s = _dot_nt(qs[e], k)
            if diagonal:
                s = jnp.where(causal, s, NEG_INF)
            s_sc[e, ki] = s
            mrun_sc[e] = jnp.maximum(mrun_sc[e], s)

    def score_step(ki, carry):
        scores(ki, False)
        return carry

    lax.fori_loop(0, qi, score_step, 0)
    scores(qi, True)
    for e in (0, 1):
        mrun_sc[e] = jnp.broadcast_to(jnp.max(mrun_sc[e], axis=-1, keepdims=True), (tq, tq))
    lrun_sc[...] = jnp.zeros(lrun_sc.shape, F32)
    acc_sc[...] = jnp.zeros(acc_sc.shape, F32)

    def value_step(ki, carry):
        v = v_ref[0, pl.ds(pl.multiple_of(ki * tq, tq), tq), :]
        for e in (0, 1):
            p = jnp.exp(s_sc[e, ki] - mrun_sc[e])
            lrun_sc[e] += p
            acc_sc[e] += _dot(p, v)
        return carry

    lax.fori_loop(0, qi + 1, value_step, 0)
    lane_o = lax.broadcasted_iota(jnp.int32, (1, 2 * MLA_V), 1)
    o0 = acc_sc[0] / jnp.sum(lrun_sc[0], axis=-1, keepdims=True)
    o1 = acc_sc[1] / jnp.sum(lrun_sc[1], axis=-1, keepdims=True)
    o_ref[0] = jnp.where(lane_o < MLA_V, o0, o1)


def _mla_attn(qa, ka, va, tq=512):
    n, s, _ = qa.shape
    npair = MLA_HEADS // 2
    return pl.pallas_call(
        functools.partial(_mla_attn_body, tq=tq),
        grid=(n, npair, s // tq),
        in_specs=[pl.BlockSpec((1, tq, MLA_PAIR), lambda b, p, i: (b, i, p)),
                  pl.BlockSpec((1, s, MLA_PAIR), lambda b, p, i: (b, 0, p)),
                  pl.BlockSpec((1, s, 2 * MLA_V), lambda b, p, i: (b, 0, p))],
        out_specs=pl.BlockSpec((1, tq, 2 * MLA_V), lambda b, p, i: (b, i, p)),
        out_shape=jax.ShapeDtypeStruct((n, s, MLA_HEADS * MLA_V), F32),
        scratch_shapes=[pltpu.VMEM((2, s // tq, tq, tq), F32), pltpu.VMEM((2, tq, tq), F32),
                        pltpu.VMEM((2, tq, tq), F32), pltpu.VMEM((2, tq, 2 * MLA_V), F32)],
        compiler_params=_cparams("parallel", "parallel", "arbitrary"), name="mla_attn",
    )(qa, ka, va)


def _top_blocks(gate, eligible, axis):
    idx = lax.broadcasted_iota(jnp.int32, gate.shape, axis)
    big = jnp.int32(1 << 20)
    g = jnp.where(eligible, gate, NEG_INF)
    sel = jnp.zeros(gate.shape, jnp.bool_)
    for _ in range(MOBA_TOPK):
        mx = jnp.max(g, axis=axis, keepdims=True)
        cand = (g == mx) & (mx > NEG_INF)
        first = jnp.min(jnp.where(cand, idx, big), axis=axis, keepdims=True)
        pick = idx == first
        sel = sel | pick
        g = jnp.where(pick, NEG_INF, g)
    return sel


def _moba_attn_body(q_ref, k_ref, vt_ref, km_ref, o_ref, s_sc, sel_sc, mrun_sc, lrun_sc, acc_sc, *, tq):
    qi = pl.program_id(2)
    q = q_ref[0]
    lane = lax.broadcasted_iota(jnp.int32, (1, 2 * MOBA_HD), 1)
    nblk = km_ref.shape[1]
    blk = lax.broadcasted_iota(jnp.int32, (nblk, tq), 0)
    km_hi, km_lo = _split(km_ref[0])
    qs = []
    for e in (0, 1):
        qe = jnp.where((lane >= MOBA_HD * e) & (lane < MOBA_HD * (e + 1)), q, 0.0)
        qs.append((qe * (MOBA_HD ** -0.5)).astype(BF16))
        q_hi, q_lo = _split(qe)
        nt = (((1,), (1,)), ((), ()))
        gate = (lax.dot_general(km_hi, q_hi, nt, preferred_element_type=F32)
                + lax.dot_general(km_lo, q_hi, nt, preferred_element_type=F32)
                + lax.dot_general(km_hi, q_lo, nt, preferred_element_type=F32))
        sel_sc[e] = _top_blocks(gate, blk < qi, 0).astype(F32)
    causal = (lax.broadcasted_iota(jnp.int32, (tq, tq), 0)
              <= lax.broadcasted_iota(jnp.int32, (tq, tq), 1))

    def scores(blocks, own):
        new = [[], []]
        for j in blocks:
            k = k_ref[0, pl.ds(pl.multiple_of(j * tq, tq), tq), :].astype(BF16)
            for e in (0, 1):
                ok = causal if own else sel_sc[e, pl.ds(j, 1), :] > 0.5
                s = jnp.where(ok, _dot_nt(k, qs[e]), NEG_INF)
                s_sc[e, j] = s
                new[e].append(s)
        for e in (0, 1):
            top = functools.reduce(jnp.maximum, new[e])
            mrun_sc[e] = top if own else jnp.maximum(mrun_sc[e], top)

    def score_pair(t, carry):
        scores((2 * t, 2 * t + 1), False)
        return carry

    scores((qi,), True)
    lax.fori_loop(0, qi // 2, score_pair, 0)

    @pl.when(qi % 2 == 1)
    def _():
        scores((qi - 1,), False)

    for e in (0, 1):
        mrun_sc[e] = jnp.broadcast_to(jnp.max(mrun_sc[e], axis=0, keepdims=True), (tq, tq))
    lrun_sc[...] = jnp.zeros(lrun_sc.shape, F32)
    acc_sc[...] = jnp.zeros(acc_sc.shape, F32)

    def values(blocks):
        vts = [vt_ref[0, 0, j].astype(BF16) for j in blocks]
        for e in (0, 1):
            ps = [jnp.exp(s_sc[e, j] - mrun_sc[e]) for j in blocks]
            lrun_sc[e] += functools.reduce(jnp.add, ps)
            acc_sc[e] += functools.reduce(jnp.add, [
                jnp.dot(vt, p.astype(BF16), preferred_element_type=F32) for vt, p in zip(vts, ps)])

    def value_pair(t, carry):
        values((2 * t, 2 * t + 1))
        return carry

    lax.fori_loop(0, (qi + 1) // 2, value_pair, 0)

    @pl.when(qi % 2 == 0)
    def _():
        values((qi,))
    o0 = acc_sc[0] / jnp.sum(lrun_sc[0], axis=0, keepdims=True)
    o1 = acc_sc[1] / jnp.sum(lrun_sc[1], axis=0, keepdims=True)
    drow = lax.broadcasted_iota(jnp.int32, (2 * MOBA_HD, 1), 0)
    o_ref[0] = jnp.where(drow < MOBA_HD, o0, o1).T


def _moba_attn(qb, kb, vb, km):
    n, s, hb = qb.shape
    tq = MOBA_BLOCK
    nblk = s // tq
    npair = hb // LANE
    vt = vb.reshape(n, nblk, tq, npair, LANE).transpose(0, 3, 1, 4, 2)
    return pl.pallas_call(
        functools.partial(_moba_attn_body, tq=tq),
        grid=(n, npair, nblk),
        in_specs=[pl.BlockSpec((1, tq, LANE), lambda b, p, i: (b, i, p)),
                  pl.BlockSpec((1, s, LANE), lambda b, p, i: (b, 0, p)),
                  pl.BlockSpec((1, 1, nblk, LANE, tq), lambda b, p, i: (b, p, 0, 0, 0)),
                  pl.BlockSpec((1, nblk, LANE), lambda b, p, i: (b, 0, p))],
        out_specs=pl.BlockSpec((1, tq, LANE), lambda b, p, i: (b, i, p)),
        out_shape=jax.ShapeDtypeStruct((n, s, hb), F32),
        scratch_shapes=[pltpu.VMEM((2, nblk, tq, tq), F32), pltpu.VMEM((2, nblk, tq), F32),
                        pltpu.VMEM((2, tq, tq), F32), pltpu.VMEM((2, tq, tq), F32),
                        pltpu.VMEM((2, LANE, tq), F32)],
        compiler_params=_cparams("parallel", "parallel", "arbitrary"), name="moba_attn",
    )(qb, kb, vt, km)


def _mm2_res_body(a_ref, b_ref, wa_ref, wb_ref, r_ref, o_ref):
    o_ref[...] = r_ref[...] + _dot(a_ref[...], wa_ref[...]) + _dot(b_ref[...], wb_ref[...])


def _mm2_res(a, b, wa, wb, res, tm):
    m, d = res.shape
    row = lambda w: pl.BlockSpec((tm, w), lambda i: (i, 0))
    return pl.pallas_call(
        _mm2_res_body, grid=(m // tm,),
        in_specs=[row(a.shape[1]), row(b.shape[1]), _const_spec(wa.shape), _const_spec(wb.shape), row(d)],
        out_specs=row(d), out_shape=jax.ShapeDtypeStruct((m, d), F32),
        compiler_params=_cparams("parallel"), name="mix_out",
    )(a, b, wa, wb, res)


def _mem_kv_body(mem_ref, gm_ref, wk_ref, wv_ref, gk_ref, k_ref, v_ref):
    mn = _rms(mem_ref[0], gm_ref[...])
    k = _dot(mn, wk_ref[...])
    for h in range(MEM_HEADS):
        sl = slice(h * MEM_HD, (h + 1) * MEM_HD)
        k_ref[0, :, sl] = _rms(k[:, sl], gk_ref[...])
    v_ref[0] = _dot(mn, wv_ref[...])


def _mem_kv(mem, g_m, w_k, w_v, g_k):
    n, ml, d = mem.shape
    hd = MEM_HEADS * MEM_HD
    blk = lambda w: pl.BlockSpec((1, ml, w), lambda b: (b, 0, 0))
    return pl.pallas_call(
        _mem_kv_body, grid=(n,),
        in_specs=[blk(d), _const_spec((1, d)), _const_spec((d, hd)), _const_spec((d, hd)),
                  _const_spec((1, MEM_HD))],
        out_specs=[blk(hd), blk(hd)],
        out_shape=[jax.ShapeDtypeStruct((n, ml, hd), F32)] * 2,
        compiler_params=_cparams("parallel"), name="mem_kv",
    )(mem, g_m[None], w_k.reshape(d, hd).astype(BF16), w_v.reshape(d, hd).astype(BF16), g_k[None])


def _mem_attn_body(x_ref, g_ref, wq_ref, gq_ref, k_ref, v_ref, wo_ref, o_ref):
    x = x_ref[0]
    q = _dot(_rms(x, g_ref[...]), wq_ref[...])
    k = k_ref[0].astype(BF16)
    v = v_ref[0].astype(BF16)
    outs = []
    for h in range(MEM_HEADS):
        sl = slice(h * MEM_HD, (h + 1) * MEM_HD)
        qh = _rms(q[:, sl], gq_ref[...])
        s = _dot_nt(qh, k[:, sl]) * (MEM_HD ** -0.5)
        p = jnp.exp(s - jnp.max(s, axis=-1, keepdims=True))
        p = p / jnp.sum(p, axis=-1, keepdims=True)
        outs.append(_dot(p, v[:, sl]))
    o_ref[0] = x + _dot(jnp.concatenate(outs, axis=-1), wo_ref[...])


def _mem_attn(x, g, w_q, g_q, k, v, w_o, tm):
    n, s, d = x.shape
    ml, hd = k.shape[1:]
    return pl.pallas_call(
        _mem_attn_body, grid=(n, s // tm),
        in_specs=[pl.BlockSpec((1, tm, d), lambda b, i: (b, i, 0)), _const_spec((1, d)),
                  _const_spec((d, hd)), _const_spec((1, MEM_HD)),
                  pl.BlockSpec((1, ml, hd), lambda b, i: (b, 0, 0)),
                  pl.BlockSpec((1, ml, hd), lambda b, i: (b, 0, 0)), _const_spec((hd, d))],
        out_specs=pl.BlockSpec((1, tm, d), lambda b, i: (b, i, 0)),
        out_shape=jax.ShapeDtypeStruct((n, s, d), F32),
        compiler_params=_cparams("parallel", "parallel"), name="mem_attn",
    )(x, g[None], w_q.reshape(d, hd).astype(BF16), g_q[None], k, v, w_o.reshape(hd, d).astype(BF16))


def _mem_attn_sample_body(x_ref, g_ref, wq_ref, gq_ref, k_ref, v_ref, wo_ref, o_ref, q_sc, att_sc, *, bs):
    x = x_ref[...]
    q = _dot(_rms(x, g_ref[...]), wq_ref[...])
    for h in range(MEM_HEADS):
        q_sc[:, h, :] = _rms(q[:, h * MEM_HD:(h + 1) * MEM_HD], gq_ref[...]) * (MEM_HD ** -0.5)
    for b in range(bs):
        s = jnp.sum(k_ref[b] * q_sc[b], axis=-1, keepdims=True)
        p = jnp.exp(s - jnp.max(s, axis=0, keepdims=True))
        o = jnp.sum(p * v_ref[b], axis=0) / jnp.sum(p, axis=0)
        for h in range(MEM_HEADS):
            att_sc[b:b + 1, h * MEM_HD:(h + 1) * MEM_HD] = o[h:h + 1, :]
    o_ref[...] = x + _dot(att_sc[...], wo_ref[...])


def _mem_attn_sample(x, g, w_q, g_q, cache_k, cache_v, layer, w_o, bs=8):
    b, d = x.shape
    ml = cache_k.shape[2]
    hd = MEM_HEADS * MEM_HD
    mem = pl.BlockSpec((None, bs, ml, MEM_HEADS, MEM_HD), lambda i: (layer, i, 0, 0, 0))
    return pl.pallas_call(
        functools.partial(_mem_attn_sample_body, bs=bs), grid=(b // bs,),
        in_specs=[pl.BlockSpec((bs, d), lambda i: (i, 0)), _const_spec((1, d)),
                  _const_spec((d, hd)), _const_spec((1, MEM_HD)), mem, mem, _const_spec((hd, d))],
        out_specs=pl.BlockSpec((bs, d), lambda i: (i, 0)),
        out_shape=jax.ShapeDtypeStruct((b, d), F32),
        scratch_shapes=[pltpu.VMEM((bs, MEM_HEADS, MEM_HD), F32), pltpu.VMEM((bs, hd), F32)],
        compiler_params=_cparams("parallel"), name="mem_attn_sample",
    )(x, g[None], w_q.reshape(d, hd).astype(BF16), g_q[None], cache_k, cache_v,
      w_o.reshape(hd, d).astype(BF16))


def _ffn_body(x_ref, g_ref, w1_ref, w2_ref, o_ref, xn_sc, acc_sc):
    j = pl.program_id(1)

    @pl.when(j == 0)
    def _():
        xn_sc[...] = _rms(x_ref[...], g_ref[...]).astype(BF16)
        acc_sc[...] = jnp.zeros(acc_sc.shape, F32)

    h = jnp.maximum(jnp.dot(xn_sc[...], w1_ref[...], preferred_element_type=F32), 0.0)
    acc_sc[...] += _dot(h * h, w2_ref[...])

    @pl.when(j == pl.num_programs(1) - 1)
    def _():
        o_ref[...] = x_ref[...] + acc_sc[...]


def _ffn(x, g, w1, w2, tm, tf=512):
    m, d = x.shape
    ff = w1.shape[1]
    return pl.pallas_call(
        _ffn_body, grid=(m // tm, ff // tf),
        in_specs=[pl.BlockSpec((tm, d), lambda i, j: (i, 0)), _const_spec((1, d)),
                  pl.BlockSpec((d, tf), lambda i, j: (0, j)), pl.BlockSpec((tf, d), lambda i, j: (j, 0))],
        out_specs=pl.BlockSpec((tm, d), lambda i, j: (i, 0)),
        out_shape=jax.ShapeDtypeStruct((m, d), F32),
        scratch_shapes=[pltpu.VMEM((tm, d), BF16), pltpu.VMEM((tm, d), F32)],
        compiler_params=_cparams("parallel", "arbitrary"), name="ffn",
    )(x, g[None], w1, w2)


def _odd_weights(w_in, w_g, b_g, g_o, w_pool, pool_scale, w_out):
    nk = GLA_HEADS * GLA_DK
    nv = GLA_HEADS * GLA_DV
    pad = GLA_DVP - GLA_DV

    def pad_heads_cols(w):
        return jnp.pad(w.reshape(w.shape[0], GLA_HEADS, GLA_DV), ((0, 0), (0, 0), (0, pad))).reshape(
            w.shape[0], GLA_HEADS * GLA_DVP)

    o_v, o_r, o_g = 2 * nk, 2 * nk + nv, 2 * nk + 2 * nv
    o_u = o_g + GLA_GATE_RANK
    win = jnp.concatenate([
        w_in[:, :o_v], pad_heads_cols(w_in[:, o_v:o_r]), pad_heads_cols(w_in[:, o_r:o_g]),
        w_in[:, o_u:], jnp.pad(w_in[:, o_g:o_u], ((0, 0), (0, LANE - GLA_GATE_RANK)))], axis=1)
    wg = jnp.pad(w_g, ((0, LANE - GLA_GATE_RANK), (0, 0)))
    go = jnp.tile(jnp.pad(g_o, (0, pad)), GLA_HEADS)[None]
    wpool = jnp.zeros((POOL_DIM, POOL_DIM), F32)
    for gi in range(len(POOL_WINDOWS)):
        lo = gi * POOL_GDIM
        wpool = wpool.at[lo:lo + POOL_GDIM, lo:lo + POOL_GDIM].set(w_pool[gi])
    wo_gla = jnp.pad(w_out[:nv].reshape(GLA_HEADS, GLA_DV, -1), ((0, 0), (0, pad), (0, 0))).reshape(
        GLA_HEADS * GLA_DVP, -1)
    return dict(win=win.astype(BF16), wg=wg.astype(BF16), bg=b_g[None], go=go, wpool=wpool.astype(BF16),
                pscale=pool_scale[None], wo_gla=wo_gla.astype(BF16), wo_pool=w_out[nv:].astype(BF16))


def _odd_proj_body(x_ref, gmix_ref, win_ref, wg_ref, bg_ref, q_ref, k_ref, v_ref, r_ref, u_ref, g_ref):
    xn = _rms(x_ref[...], gmix_ref[...])
    z = _dot(xn, win_ref[...])
    nk = GLA_HEADS * GLA_DK
    nvp = GLA_HEADS * GLA_DVP
    q_ref[...] = z[:, :nk] * (GLA_DK ** -0.5)
    k_ref[...] = z[:, nk:2 * nk]
    v_ref[...] = z[:, 2 * nk:2 * nk + nvp]
    r_ref[...] = z[:, 2 * nk + nvp:2 * nk + 2 * nvp]
    o_u = 2 * nk + 2 * nvp
    u_ref[...] = z[:, o_u:o_u + POOL_DIM]
    zg = _dot(z[:, o_u + POOL_DIM:o_u + POOL_DIM + LANE], wg_ref[...]) + bg_ref[...]
    g_ref[...] = (jnp.minimum(zg, 0.0) - jnp.log1p(jnp.exp(-jnp.abs(zg)))) * (1.0 / GLA_TAU)


def _odd_proj(x, g_mix, ow, tm):
    m, d = x.shape
    nk = GLA_HEADS * GLA_DK
    nvp = GLA_HEADS * GLA_DVP
    row = lambda w: pl.BlockSpec((tm, w), lambda i: (i, 0))
    widths = [nk, nk, nvp, nvp, POOL_DIM, nk]
    return pl.pallas_call(
        _odd_proj_body, grid=(m // tm,),
        in_specs=[row(d), _const_spec((1, d)), _const_spec(ow["win"].shape), _const_spec(ow["wg"].shape),
                  _const_spec((1, nk))],
        out_specs=[row(w) for w in widths],
        out_shape=[jax.ShapeDtypeStruct((m, w), F32) for w in widths],
        compiler_params=_cparams("parallel"), name="odd_proj",
    )(x, g_mix[None], ow["win"], ow["wg"], ow["bg"])


def _gla_out(o, r, go):
    outs = []
    for h in range(GLA_HEADS):
        sl = slice(h * GLA_DVP, (h + 1) * GLA_DVP)
        oh = o[h] if isinstance(o, (list, tuple)) else o[:, sl]
        ms = jnp.sum(oh * oh, axis=-1, keepdims=True) * (1.0 / GLA_DV)
        rh = r[:, sl]
        outs.append(oh * lax.rsqrt(ms + EPS) * go[:, sl] * (rh / (1.0 + jnp.exp(-rh))))
    return jnp.concatenate(outs, axis=-1)


def _gla_scan_body(q_ref, k_ref, v_ref, g_ref, r_ref, go_ref, o_ref, st_ref, st_sc):
    c = pl.program_id(0)

    @pl.when(c == 0)
    def _():
        st_sc[...] = jnp.zeros(st_sc.shape, F32)

    nb, cs = q_ref.shape[:2]
    row = lax.broadcasted_iota(jnp.int32, (cs, cs), 0)
    col = lax.broadcasted_iota(jnp.int32, (cs, cs), 1)
    tril = (col <= row)
    tril_bf = tril.astype(BF16)
    for b in range(nb):
        outs = []
        for h in range(GLA_HEADS):
            ks = slice(h * GLA_DK, (h + 1) * GLA_DK)
            vs = slice(h * GLA_DVP, (h + 1) * GLA_DVP)
            q, k, v, g = q_ref[b, :, ks], k_ref[b, :, ks], v_ref[b, :, vs], g_ref[b, :, ks]
            g_hi = g.astype(BF16)
            g_r = g - g_hi.astype(F32)
            g_mid = g_r.astype(BF16)
            g_lo = (g_r - g_mid.astype(F32)).astype(BF16)
            gc = (jnp.dot(tril_bf, g_hi, preferred_element_type=F32)
                  + jnp.dot(tril_bf, g_mid, preferred_element_type=F32)
                  + jnp.dot(tril_bf, g_lo, preferred_element_type=F32))
            g_last = gc[cs - 1:cs, :]
            qd = q * jnp.exp(gc)
            a = jnp.where(tril, _dot_nt(qd, k * jnp.exp(-gc)), 0.0)
            st = st_sc[b, h]
            outs.append(_dot_nt(qd, st) + _dot(a, v))
            st_sc[b, h] = st * jnp.exp(g_last) + _dot_tn(v, k * jnp.exp(g_last - gc))
        o_ref[b] = _gla_out(outs, r_ref[b], go_ref[...])

    @pl.when(c == pl.num_programs(0) - 1)
    def _():
        st_ref[...] = st_sc[...]


def _gla_scan(q, k, v, g, r, go):
    n, s, nk = q.shape
    nvp = v.shape[2]
    cs = GLA_CHUNK
    blk = lambda w: pl.BlockSpec((n, cs, w), lambda c: (0, c, 0))
    st_shape = (n, GLA_HEADS, GLA_DVP, GLA_DK)
    return pl.pallas_call(
        _gla_scan_body, grid=(s // cs,),
        in_specs=[blk(nk), blk(nk), blk(nvp), blk(nk), blk(nvp), _const_spec((1, nvp))],
        out_specs=[blk(nvp), _const_spec(st_shape)],
        out_shape=[jax.ShapeDtypeStruct((n, s, nvp), F32), jax.ShapeDtypeStruct(st_shape, F32)],
        scratch_shapes=[pltpu.VMEM(st_shape, F32)],
        compiler_params=_cparams("arbitrary"), name="gla_scan",
    )(q, k, v, g, r, go)


def _pool_group_select(vals):
    grp = lax.broadcasted_iota(jnp.int32, (1, POOL_DIM), 1) // POOL_GDIM
    out = vals[-1]
    for gi in range(len(vals) - 2, -1, -1):
        out = jnp.where(grp == gi, vals[gi], out)
    return out


def _pool_body(u_ref, wp_ref, ps_ref, o_ref, ext_sc, *, tm):
    i = pl.program_id(1)
    hist = POOL_HIST + 1

    @pl.when(i == 0)
    def _():
        ext_sc[0:hist, :] = jnp.zeros((hist, POOL_DIM), F32)

    u = u_ref[0]
    ext_sc[hist:hist + tm, :] = u
    acc = u
    sums = []
    for k in range(1, max(POOL_WINDOWS)):
        acc = acc + ext_sc[hist - k:hist - k + tm, :]
        if k + 1 in POOL_WINDOWS:
            sums.append(acc)
    pos = i * tm + lax.broadcasted_iota(jnp.int32, (tm, 1), 0)
    win = _pool_group_select([jnp.full((1, POOL_DIM), w, jnp.int32) for w in POOL_WINDOWS])
    cnt = jnp.minimum(pos + 1, win).astype(F32)
    diff = _pool_group_select(sums) / cnt - u
    o_ref[0] = _dot(diff, wp_ref[...]) * ps_ref[...]
    ext_sc[0:hist, :] = ext_sc[tm:tm + hist, :]


def _pool_prompt(u, wpool, pscale, tm=512):
    n, s, pd = u.shape
    blk = pl.BlockSpec((1, tm, pd), lambda b, i: (b, i, 0))
    return pl.pallas_call(
        functools.partial(_pool_body, tm=tm), grid=(n, s // tm),
        in_specs=[blk, _const_spec((pd, pd)), _const_spec((1, pd))],
        out_specs=blk, out_shape=jax.ShapeDtypeStruct((n, s, pd), F32),
        scratch_shapes=[pltpu.VMEM((tm + POOL_HIST + 1, pd), F32)],
        compiler_params=_cparams("parallel", "arbitrary"), name="pool_prompt",
    )(u, wpool, pscale)


def _pool_sample_body(ue_ref, wp_ref, ps_ref, o_ref, *, past):
    ue = ue_ref[...]
    t = ue.shape[1]
    rowi = lax.broadcasted_iota(jnp.int32, (1, t, 1), 1)
    sums = [jnp.sum(jnp.where(rowi >= t - w, ue, 0.0), axis=1) for w in POOL_WINDOWS]
    cnt = _pool_group_select(
        [jnp.full((1, POOL_DIM), float(min(past + 1, w)), F32) for w in POOL_WINDOWS])
    diff = _pool_group_select(sums) / cnt - ue[:, t - 1, :]
    o_ref[...] = _dot(diff, wp_ref[...]) * ps_ref[...]


def _pool_sample(u_ext, wpool, pscale, past):
    b, t, pd = u_ext.shape
    return pl.pallas_call(
        functools.partial(_pool_sample_body, past=past), grid=(1,),
        in_specs=[_const_spec((b, t, pd)), _const_spec((pd, pd)), _const_spec((1, pd))],
        out_specs=_const_spec((b, pd)), out_shape=jax.ShapeDtypeStruct((b, pd), F32),
        compiler_params=_cparams("arbitrary"), name="pool_sample",
    )(u_ext, wpool, pscale)


def _gla_sample_body(q_ref, k_ref, v_ref, g_ref, r_ref, go_ref, st_ref, o_ref, sn_ref, o_sc, *, bs):
    eye = (lax.broadcasted_iota(jnp.int32, (GLA_DK, GLA_DK), 0)
           == lax.broadcasted_iota(jnp.int32, (GLA_DK, GLA_DK), 1))

    def col(rowv):
        return jnp.sum(jnp.where(eye, rowv, 0.0), axis=1, keepdims=True)

    o_sc[...] = jnp.zeros(o_sc.shape, F32)
    for b in range(bs):
        for h in range(GLA_HEADS):
            ks = slice(h * GLA_DK, (h + 1) * GLA_DK)
            q = q_ref[b:b + 1, ks]
            k = k_ref[b:b + 1, ks]
            eg = jnp.exp(g_ref[b:b + 1, ks])
            v = v_ref[b:b + 1, h * GLA_DVP:h * GLA_DVP + GLA_DV]
            s_new = st_ref[b, h] * col(eg) + col(k) * v
            sn_ref[b, h] = s_new
            o_sc[b:b + 1, h * GLA_DVP:h * GLA_DVP + GLA_DV] = jnp.sum(col(q) * s_new, axis=0, keepdims=True)
    o_ref[...] = _gla_out(o_sc[...], r_ref[...], go_ref[...])


def _gla_sample(q, k, v, g, r, go, state, bs=8):
    b, nk = q.shape
    nvp = v.shape[1]
    row = lambda w: pl.BlockSpec((bs, w), lambda i: (i, 0))
    st = pl.BlockSpec((bs, GLA_HEADS, GLA_DK, GLA_DV), lambda i: (i, 0, 0, 0))
    return pl.pallas_call(
        functools.partial(_gla_sample_body, bs=bs), grid=(b // bs,),
        in_specs=[row(nk), row(nk), row(nvp), row(nk), row(nvp), _const_spec((1, nvp)), st],
        out_specs=[row(nvp), st],
        out_shape=[jax.ShapeDtypeStruct((b, nvp), F32), jax.ShapeDtypeStruct(state.shape, F32)],
        scratch_shapes=[pltpu.VMEM((bs, nvp), F32)],
        compiler_params=_cparams("parallel"), name="gla_sample",
    )(q, k, v, g, r, go, state)


def _sample_q_body(qa_ref, gak_ref, wk_ref, selr_ref, selsw_ref, qabs_ref, qr_ref, qsw_ref):
    qa = qa_ref[...]
    lane = lax.broadcasted_iota(jnp.int32, (1, qa.shape[1]), 1)
    qg = qa.astype(F32) * gak_ref[...]
    for h in range(MLA_HEADS):
        base = MLA_PAIR * (h // 2) + MLA_NOPE * (h % 2)
        nope = (lane >= base) & (lane < base + MLA_NOPE)
        qabs_ref[h] = _dot_nt(jnp.where(nope, qg, 0.0), wk_ref[...]).astype(BF16)
        qr_ref[h] = jnp.dot(qa, selr_ref[h], preferred_element_type=F32).astype(BF16)
        qsw_ref[h] = jnp.dot(qa, selsw_ref[h], preferred_element_type=F32).astype(BF16)


def _sample_q(qa, gak, wk):
    b, width = qa.shape
    cols = _mla_cols()
    selr = np.zeros((MLA_HEADS, width, MLA_ROPE), np.float32)
    selsw = np.zeros((MLA_HEADS, width, MLA_ROPE), np.float32)
    half = MLA_ROPE // 2
    for h in range(MLA_HEADS):
        rc = cols[h, MLA_NOPE:]
        selr[h, rc, np.arange(MLA_ROPE)] = 1.0
        selsw[h, rc[half:], np.arange(half)] = 1.0
        selsw[h, rc[:half], half + np.arange(half)] = -1.0
    outs = pl.pallas_call(
        _sample_q_body, grid=(1,),
        in_specs=[_const_spec(qa.shape), _const_spec(gak.shape), _const_spec(wk.shape),
                  _const_spec(selr.shape), _const_spec(selsw.shape)],
        out_specs=[_const_spec((MLA_HEADS, b, MLA_KV_LORA)), _const_spec((MLA_HEADS, b, MLA_ROPE)),
                   _const_spec((MLA_HEADS, b, MLA_ROPE))],
        out_shape=[jax.ShapeDtypeStruct((MLA_HEADS, b, MLA_KV_LORA), BF16),
                   jax.ShapeDtypeStruct((MLA_HEADS, b, MLA_ROPE), BF16),
                   jax.ShapeDtypeStruct((MLA_HEADS, b, MLA_ROPE), BF16)],
        compiler_params=_cparams("arbitrary"), name="sample_q",
    )(qa, gak, wk, jnp.asarray(selr, BF16), jnp.asarray(selsw, BF16))
    return [jnp.swapaxes(o, 0, 1) for o in outs]


def _paged_body(pt_ref, ckv_hbm, kr_hbm, k_hbm, v_hbm,
                qabs_ref, qr_ref, qsw_ref, qb_ref, ckvn_ref, krn_ref, tc_ref, ts_ref, tcn_ref, tsn_ref,
                wuk_ref, e64_ref, olat_ref, part_ref,
                ckv_buf, kr_buf, k_buf, v_buf, sem, m_sc, l_sc, acc_sc, qcol_sc, *, npg, layer):
    i = pl.program_id(0)
    c = pl.program_id(1)
    nc = pl.num_programs(1)
    step = i * nc + c
    slot = step % 2
    caches = ((ckv_hbm, ckv_buf), (kr_hbm, kr_buf), (k_hbm, k_buf), (v_hbm, v_buf))

    def page_copies(ii, cc, sl):
        out = []
        for j in range(npg):
            pg = pt_ref[ii, cc * npg + j]
            for a, (hbm, buf) in enumerate(caches):
                out.append(pltpu.make_async_copy(hbm.at[layer, pg], buf.at[sl, j], sem.at[sl, a]))
        return out

    @pl.when(step == 0)
    def _():
        for cp in page_copies(i, c, slot):
            cp.start()

    @pl.when(step + 1 < pl.num_programs(0) * nc)
    def _():
        wrap = c + 1 == nc
        for cp in page_copies(jnp.where(wrap, i + 1, i), jnp.where(wrap, 0, c + 1), 1 - slot):
            cp.start()

    for cp in page_copies(i, c, slot):
        cp.wait()

    class _Page:
        def __init__(self, buf, j):
            self.buf, self.j = buf, j

        def __getitem__(self, idx):
            if idx is Ellipsis:
                return self.buf[slot, self.j]
            return self.buf[slot, self.j, idx]

    pages = [_Page(buf, j) for j in range(npg) for _, buf in caches]
    rows = MOBA_HEADS
    eye = (lax.broadcasted_iota(jnp.int32, (MOBA_HD, LANE), 0)
           == lax.broadcasted_iota(jnp.int32, (MOBA_HD, LANE), 1))

    @pl.when(c == 0)
    def _():
        m_sc[...] = jnp.full(m_sc.shape, NEG_INF, F32)
        l_sc[...] = jnp.zeros(l_sc.shape, F32)
        acc_sc[...] = jnp.zeros(acc_sc.shape, F32)
        for h in range(MOBA_HEADS):
            qcol_sc[h] = jnp.broadcast_to(
                jnp.sum(jnp.where(eye, qb_ref[0, h:h + 1, :], 0.0), axis=1, keepdims=True), (MOBA_HD, LANE))

    qr, qsw = qr_ref[0], qsw_ref[0]

    def mla_update(ckv_b, kr_t, tc_t, ts_t, n_valid):
        kn = jnp.dot(ckv_b, wuk_ref[...], preferred_element_type=F32)
        mix = (jnp.dot(ckv_b, qabs_ref[0], preferred_element_type=F32)
               + jnp.dot((kn * kn).astype(BF16), e64_ref[...], preferred_element_type=F32))
        mix_t = mix.T
        ss = mix_t[rows:2 * rows] + jnp.sum(kr_t * kr_t, axis=0, keepdims=True)
        s = mix_t[0:rows] + _dot(qr, kr_t * tc_t) + _dot(qsw, kr_t * ts_t)
        s = s * lax.rsqrt(ss * (1.0 / MLA_QK) + EPS)
        if n_valid is not None:
            s = jnp.where(lax.broadcasted_iota(jnp.int32, s.shape, 1) < n_valid, s, NEG_INF)
        m_prev = m_sc[...]
        m_new = jnp.maximum(m_prev, jnp.max(s, axis=-1, keepdims=True))
        alpha = jnp.exp(m_prev - m_new)
        p = jnp.exp(s - m_new)
        l_sc[...] = alpha * l_sc[...] + jnp.sum(p, axis=-1, keepdims=True)
        acc_sc[...] = alpha * acc_sc[...] + jnp.dot(p.astype(BF16), ckv_b, preferred_element_type=F32)
        m_sc[...] = m_new

    mla_update(jnp.concatenate([pages[4 * j][...].astype(BF16) for j in range(npg)], axis=0),
               jnp.concatenate([pages[4 * j + 1][...] for j in range(npg)], axis=1),
               tc_ref[...], ts_ref[...], None)

    rowi = lax.broadcasted_iota(jnp.int32, (rows, LANE), 0)
    lanei = lax.broadcasted_iota(jnp.int32, (rows, LANE), 1)
    ppb = MOBA_BLOCK // PAGE_SIZE
    for g in range(npg // ppb):
        pg = [g * ppb + t for t in range(ppb)]
        raw = []
        for j in pg:
            s8 = jnp.zeros((rows, LANE), F32)
            for h in range(MOBA_HEADS):
                sh = jnp.sum(pages[4 * j + 2][h] * qcol_sc[h], axis=0, keepdims=True)
                s8 = jnp.where(rowi == h, sh, s8)
            raw.append(s8)
        gate = sum(jnp.sum(r, axis=-1, keepdims=True) for r in raw) * (1.0 / MOBA_BLOCK)
        sc = [r * (MOBA_HD ** -0.5) for r in raw]
        m = functools.reduce(jnp.maximum, [jnp.max(t, axis=-1, keepdims=True) for t in sc])
        p = [jnp.exp(t - m) for t in sc]
        l = sum(jnp.sum(t, axis=-1, keepdims=True) for t in p)
        o8 = jnp.zeros((rows, LANE), F32)
        for h in range(MOBA_HEADS):
            pv = sum(p[t][h:h + 1, :] * pages[4 * j + 3][h] for t, j in enumerate(pg))
            ocol = jnp.sum(pv, axis=1, keepdims=True)
            o8 = jnp.where(rowi == h, jnp.sum(jnp.where(eye, ocol, 0.0), axis=0, keepdims=True), o8)
        part_ref[0, g] = jnp.where(lanei == MOBA_HD, m,
                                   jnp.where(lanei == MOBA_HD + 1, l,
                                             jnp.where(lanei == MOBA_HD + 2, gate, o8)))

    @pl.when(c == pl.num_programs(1) - 1)
    def _():
        mla_update(ckvn_ref[0].astype(BF16), krn_ref[0], tcn_ref[...], tsn_ref[...], 1)
        olat_ref[0] = acc_sc[...] / l_sc[...]


def _paged_attend(page_table, layer, c_ckv, c_kr, c_k, c_v, qabs, qr, qsw, qb, ckv_new, kr_new, gk_rope,
                  wuk, npg=16):
    b, n_pages = page_table.shape
    past = n_pages * PAGE_SIZE
    rows = MOBA_HEADS
    nstep = n_pages // npg
    ppb = MOBA_BLOCK // PAGE_SIZE
    c, s = _rope_tables(jnp.arange(past + 1), MLA_ROPE // 2, MLA_THETA)
    tc = (jnp.concatenate([c, c], -1) * gk_rope[None]).T
    ts = (jnp.concatenate([s, s], -1) * gk_rope[None]).T
    e64 = np.zeros((MLA_HEADS * MLA_NOPE, LANE), np.float32)
    for h in range(MLA_HEADS):
        e64[h * MLA_NOPE:(h + 1) * MLA_NOPE, rows + h] = 1.0
    qabs_t = jnp.pad(jnp.swapaxes(qabs, 1, 2), ((0, 0), (0, 0), (0, LANE - rows)))

    caches = [c_ckv, c_kr, c_k, c_v]
    per_sample = lambda r, w: pl.BlockSpec((1, r, w), lambda i, cc, pt: (i, 0, 0))
    const = lambda shape: pl.BlockSpec(shape, lambda i, cc, pt: (0,) * len(shape))
    tab = pl.BlockSpec((MLA_ROPE, npg * PAGE_SIZE), lambda i, cc, pt: (0, cc))
    in_specs = [pl.BlockSpec(memory_space=pl.ANY)] * len(caches) + [
        per_sample(MLA_KV_LORA, LANE), per_sample(rows, MLA_ROPE), per_sample(rows, MLA_ROPE),
        per_sample(rows, LANE), per_sample(PAGE_SIZE, MLA_KV_LORA), per_sample(MLA_ROPE, PAGE_SIZE),
        tab, tab, const((MLA_ROPE, PAGE_SIZE)), const((MLA_ROPE, PAGE_SIZE)), const(wuk.shape),
        const(e64.shape)]
    grid_spec = pltpu.PrefetchScalarGridSpec(
        num_scalar_prefetch=1, grid=(b, nstep), in_specs=in_specs,
        out_specs=[per_sample(rows, MLA_KV_LORA),
                   pl.BlockSpec((1, npg // ppb, rows, LANE), lambda i, cc, pt: (i, cc, 0, 0))],
        scratch_shapes=[pltpu.VMEM((2, npg) + t.shape[2:], t.dtype) for t in caches]
        + [pltpu.SemaphoreType.DMA((2, len(caches))),
           pltpu.VMEM((rows, 1), F32), pltpu.VMEM((rows, 1), F32),
           pltpu.VMEM((rows, MLA_KV_LORA), F32),
           pltpu.VMEM((MOBA_HEADS, MOBA_HD, LANE), F32)])
    pad_rows = lambda t: jnp.pad(t[:, None, :], ((0, 0), (0, PAGE_SIZE - 1), (0, 0)))
    pad_cols = lambda t: jnp.pad(t[:, :, None], ((0, 0), (0, 0), (0, PAGE_SIZE - 1)))
    tnew = lambda t: jnp.broadcast_to(t[:, past:past + 1], (MLA_ROPE, PAGE_SIZE))
    qb_pad = jnp.pad(qb.reshape(b, MOBA_HEADS, MOBA_HD), ((0, 0), (0, 0), (0, LANE - MOBA_HD)))
    return pl.pallas_call(
        functools.partial(_paged_body, npg=npg, layer=layer), grid_spec=grid_spec,
        out_shape=[jax.ShapeDtypeStruct((b, rows, MLA_KV_LORA), F32),
                   jax.ShapeDtypeStruct((b, n_pages // ppb, rows, LANE), F32)],
        compiler_params=_cparams("arbitrary", "arbitrary"), name="paged_attend",
    )(page_table, *caches, qabs_t, qr, qsw, qb_pad, pad_rows(ckv_new), pad_cols(kr_new),
      tc[:, :past], ts[:, :past], tnew(tc), tnew(ts), wuk, jnp.asarray(e64, BF16))


def _moba_combine_body(part_ref, q_ref, kn_ref, vn_ref, out_ref):
    part = part_ref[0]
    o = part[:, :, :MOBA_HD]
    m = part[:, :, MOBA_HD:MOBA_HD + 1]
    l = part[:, :, MOBA_HD + 1:MOBA_HD + 2]
    gate = part[:, :, MOBA_HD + 2:MOBA_HD + 3]
    sel = _top_blocks(gate, jnp.full(gate.shape, True), 0)
    s_own = jnp.sum(q_ref[0] * kn_ref[0], axis=-1, keepdims=True) * (MOBA_HD ** -0.5)
    m_sel = jnp.where(sel, m, NEG_INF)
    mt = jnp.maximum(jnp.max(m_sel, axis=0), s_own)
    w = jnp.exp(m_sel - mt)
    w_own = jnp.exp(s_own - mt)
    den = jnp.sum(w * l, axis=0) + w_own
    num = jnp.sum(w * o, axis=0) + w_own * vn_ref[0]
    out_ref[0] = num / den


def _moba_combine(part, qb, kb_new, vb_new):
    b, nb = part.shape[:2]
    hb = MOBA_HEADS * MOBA_HD
    r3 = lambda t: t.reshape(b, MOBA_HEADS, MOBA_HD)
    blk = pl.BlockSpec((1, MOBA_HEADS, MOBA_HD), lambda i: (i, 0, 0))
    out = pl.pallas_call(
        _moba_combine_body, grid=(b,),
        in_specs=[pl.BlockSpec((1, nb, MOBA_HEADS, LANE), lambda i: (i, 0, 0, 0)), blk, blk, blk],
        out_specs=blk, out_shape=jax.ShapeDtypeStruct((b, MOBA_HEADS, MOBA_HD), F32),
        compiler_params=_cparams("parallel"), name="moba_combine",
    )(part, r3(qb), r3(kb_new), r3(vb_new))
    return out.reshape(b, hb)


def _latent_out_body(olat_ref, wv_ref, o_ref):
    lane = lax.broadcasted_iota(jnp.int32, (1, MLA_HEADS * MLA_V), 1)
    wv = wv_ref[...]
    acc = jnp.zeros(o_ref.shape, F32)
    for h in range(MLA_HEADS):
        wh = jnp.where((lane >= h * MLA_V) & (lane < (h + 1) * MLA_V), wv, jnp.zeros_like(wv))
        acc = acc + _dot(olat_ref[h], wh)
    o_ref[...] = acc


def _latent_out(olat_hm, wv):
    _, b, _ = olat_hm.shape
    return pl.pallas_call(
        _latent_out_body, grid=(1,),
        in_specs=[_const_spec(olat_hm.shape), _const_spec(wv.shape)],
        out_specs=_const_spec((b, MLA_HEADS * MLA_V)),
        out_shape=jax.ShapeDtypeStruct((b, MLA_HEADS * MLA_V), F32),
        compiler_params=_cparams("arbitrary"), name="latent_out",
    )(olat_hm, wv)


def _even_layer(yp, ys, page_table, layer, c_ckv, c_kr, c_k, c_v, g_mix, w_in, g_cq, g_ckv, w_uq, w_uk, w_uv,
                g_aq, g_ak, g_bq, g_bk, w_out):
    n, s, d = yp.shape
    b = ys.shape[0]
    past = page_table.shape[1] * PAGE_SIZE
    hb = MOBA_HEADS * MOBA_HD
    ew = _even_weights(w_in, g_cq, g_ckv, w_uq, w_uk, w_uv, g_aq, g_ak, g_bq, g_bk)
    wo_a = w_out[:MLA_HEADS * MLA_V].astype(BF16)
    wo_b = w_out[MLA_HEADS * MLA_V:].astype(BF16)

    ckv, kr, qa, ka, va, qb, kb, vb, km = _even_proj(yp.reshape(n * s, d), g_mix, ew, jnp.arange(s),
                                                     MOBA_BLOCK)
    r3 = lambda t: t.reshape(n, s, t.shape[-1])
    oa = _mla_attn(r3(qa), r3(ka), r3(va))
    ob = _moba_attn(r3(qb), r3(kb), r3(vb), km.reshape(n, s // MOBA_BLOCK, hb))
    yp_new = _mm2_res(oa.reshape(n * s, -1), ob.reshape(n * s, -1), wo_a, wo_b, yp.reshape(n * s, d),
                      512).reshape(n, s, d)
    p_caches = (r3(ckv), r3(kr), kb.reshape(n, s, MOBA_HEADS, MOBA_HD), vb.reshape(n, s, MOBA_HEADS, MOBA_HD))

    sckv, skr, sqa, _, _, sqb, skb, svb, _ = _even_proj(
        ys.reshape(b, d), g_mix, ew, jnp.full((b,), past, jnp.int32), b)
    qabs, qr, qsw = _sample_q(sqa, ew["gak"], ew["wk"])
    wuk = w_uk.reshape(MLA_KV_LORA, -1).astype(BF16)
    olat, part = _paged_attend(page_table, layer, c_ckv, c_kr, c_k, c_v, qabs, qr, qsw, sqb, sckv, skr,
                               g_ak[MLA_NOPE:], wuk)
    soa = _latent_out(jnp.swapaxes(olat, 0, 1), ew["wv"])
    sob = _moba_combine(part, sqb, skb, svb)
    ys_new = _mm2_res(soa, sob, wo_a, wo_b, ys.reshape(b, d), b).reshape(b, 1, d)
    s_caches = (sckv[:, None], skr[:, None], skb.reshape(b, 1, MOBA_HEADS, MOBA_HD),
                svb.reshape(b, 1, MOBA_HEADS, MOBA_HD))
    return yp_new, ys_new, p_caches, s_caches


def _odd_layer(yp, ys, past, s_prev, u_prev, g_mix, w_in, w_g, b_g, g_o, w_pool, pool_scale, w_out):
    n, s, d = yp.shape
    b = ys.shape[0]
    ow = _odd_weights(w_in, w_g, b_g, g_o, w_pool, pool_scale, w_out)

    q, k, v, r, u, g = _odd_proj(yp.reshape(n * s, d), g_mix, ow, 512)
    r3 = lambda t: t.reshape(n, s, t.shape[-1])
    og, st_t = _gla_scan(r3(q), r3(k), r3(v), r3(g), r3(r), ow["go"])
    u3 = r3(u)
    pm = _pool_prompt(u3, ow["wpool"], ow["pscale"])
    yp_new = _mm2_res(og.reshape(n * s, -1), pm.reshape(n * s, -1), ow["wo_gla"], ow["wo_pool"],
                      yp.reshape(n * s, d), 512).reshape(n, s, d)
    p_state = jnp.swapaxes(st_t, 2, 3)[..., :GLA_DV]
    p_pool = u3[:, s - POOL_HIST:]

    sq, sk, sv, sr, su, sg = _odd_proj(ys.reshape(b, d), g_mix, ow, b)
    sog, s_new = _gla_sample(sq, sk, sv, sg, sr, ow["go"], s_prev)
    u_ext = jnp.concatenate([u_prev, su[:, None]], axis=1)
    spm = _pool_sample(u_ext, ow["wpool"], ow["pscale"], past)
    ys_new = _mm2_res(sog, spm, ow["wo_gla"], ow["wo_pool"], ys.reshape(b, d), b).reshape(b, 1, d)
    return yp_new, ys_new, (p_state, p_pool), (s_new, u_ext[:, 1:])


def kernel(x_prompt, x_sample, cache_mla_ckv, cache_mla_krope, cache_moba_k, cache_moba_v, state_gla, state_pool, cache_mem_k, cache_mem_v, page_table, mem_prompt, g_mix, g_memx, g_memm, w_mq, w_mk, w_mv, g_mq, g_mk, w_mo, g_ffn, w_ff1, w_ff2, ev_w_in, ev_g_cq, ev_g_ckv, ev_w_uq, ev_w_uk, ev_w_uv, ev_g_mla_q, ev_g_mla_k, ev_g_moba_q, ev_g_moba_k, ev_w_out, od_w_in, od_w_gate, od_b_gate, od_g_gla, od_w_pool, od_pool_scale, od_w_out):
    depth = g_mix.shape[0]
    n, s, d = x_prompt.shape
    b = x_sample.shape[0]
    past = page_table.shape[1] * PAGE_SIZE
    yp, ys = x_prompt, x_sample
    outs = {name: [] for name in ("p_ckv", "p_kr", "p_k", "p_v", "s_ckv", "s_kr", "s_k", "s_v",
                                  "p_gla", "p_pool", "s_gla", "s_pool", "p_mk", "p_mv")}
    for l in range(depth):
        i = l // 2
        if l % 2 == 0:
            yp, ys, pc, sc = _even_layer(
                yp, ys, page_table, i, cache_mla_ckv, jnp.transpose(cache_mla_krope, (0, 1, 3, 2)),
                jnp.transpose(cache_moba_k, (0, 1, 3, 4, 2)), jnp.transpose(cache_moba_v, (0, 1, 3, 4, 2)),
                g_mix[l], ev_w_in[i], ev_g_cq[i], ev_g_ckv[i], ev_w_uq[i], ev_w_uk[i], ev_w_uv[i],
                ev_g_mla_q[i], ev_g_mla_k[i], ev_g_moba_q[i], ev_g_moba_k[i], ev_w_out[i])
            for name, val in zip(("p_ckv", "p_kr", "p_k", "p_v"), pc):
                outs[name].append(val)
            for name, val in zip(("s_ckv", "s_kr", "s_k", "s_v"), sc):
                outs[name].append(val)
        else:
            yp, ys, ps, ss = _odd_layer(
                yp, ys, past, state_gla[i], state_pool[i], g_mix[l], od_w_in[i], od_w_gate[i], od_b_gate[i],
                od_g_gla[i], od_w_pool[i], od_pool_scale[i], od_w_out[i])
            for name, val in zip(("p_gla", "p_pool"), ps):
                outs[name].append(val)
            for name, val in zip(("s_gla", "s_pool"), ss):
                outs[name].append(val)
        mk, mv = _mem_kv(mem_prompt, g_memm[l], w_mk[l], w_mv[l], g_mk[l])
        ml = mk.shape[1]
        outs["p_mk"].append(mk.reshape(n, ml, MEM_HEADS, MEM_HD))
        outs["p_mv"].append(mv.reshape(n, ml, MEM_HEADS, MEM_HD))
        yp = _mem_attn(yp, g_memx[l], w_mq[l], g_mq[l], mk, mv, w_mo[l], 512)
        ys = _mem_attn_sample(ys.reshape(b, d), g_memx[l], w_mq[l], g_mq[l], cache_mem_k, cache_mem_v, l,
                              w_mo[l]).reshape(b, 1, d)
        w1 = w_ff1[l].astype(BF16)
        w2 = w_ff2[l].astype(BF16)
        yp = _ffn(yp.reshape(n * s, d), g_ffn[l], w1, w2, 1024).reshape(n, s, d)
        ys = _ffn(ys.reshape(b, d), g_ffn[l], w1, w2, b).reshape(b, 1, d)
    st = lambda name: jnp.stack(outs[name])
    return (yp, ys, st("p_ckv"), st("p_kr"), st("p_k"), st("p_v"), st("p_gla"), st("p_pool"),
            st("p_mk"), st("p_mv"), st("s_ckv"), st("s_kr"), st("s_k"), st("s_v"), st("s_gla"), st("s_pool"))
```
